```python
import math
import jax, jax.numpy as jnp
from jax import lax
import numpy as np

D_MODEL = 4096
BATCH = 2
SEQ = 8192
DEPTH = 1

ATT_HEADS = 16
HEAD_DIM = 128
ATT_WIDTH = ATT_HEADS * HEAD_DIM
DIL_PAIRS = ((128, 1), (512, 4), (2048, 16))
NUM_BUCKETS = 32
MAX_DISTANCE = 1024
POOL_WINDOWS = (2, 4, 8, 16)
POOL_WIDTH = D_MODEL - ATT_WIDTH
POOL_GROUP = POOL_WIDTH // len(POOL_WINDOWS)
MIX_WIDTH = ATT_WIDTH + POOL_WIDTH
IN_WIDTH = 3 * ATT_WIDTH + POOL_WIDTH
PEER_HEADS = 8
PEER_QDIM = 256
N_KEYS = 128
N_EXPERTS = N_KEYS * N_KEYS
PEER_TOPK = 16
PEER_CHUNK = 64
RMS_EPS = 1e-6
NEG_INF = -1e30

kernel_name = "hybrid_dilated_pool_peer_block"


def rmsnorm(x, gain):
    x32 = x.astype(jnp.float32)
    y = x32 * lax.rsqrt(jnp.mean(x32 * x32, axis=-1, keepdims=True) + RMS_EPS)
    return (y * gain.astype(jnp.float32)).astype(x.dtype)


def t5_bucket(rel):
    half = NUM_BUCKETS // 2
    n = -rel
    ret = jnp.where(n < 0, half, 0)
    n = jnp.abs(n)
    max_exact = half // 2
    nf = jnp.maximum(n, 1).astype(jnp.float32)
    large = max_exact + (jnp.log(nf / max_exact) / math.log(MAX_DISTANCE / max_exact)
                         * (half - max_exact)).astype(jnp.int32)
    large = jnp.minimum(large, half - 1)
    return ret + jnp.where(n < max_exact, n, large)


def dilated_window_branch(q, k, v, rel_bias, window, dil):
    B, S, H, Dh = q.shape
    side = window // (2 * dil)
    L = S // dil
    nb = -(-L // side)
    Lp = nb * side

    def to_res(t):
        t = t.reshape(B, L, dil, H, Dh).transpose(0, 2, 3, 1, 4)
        return jnp.pad(t, ((0, 0), (0, 0), (0, 0), (0, Lp - L), (0, 0)))

    def band(t):
        t = jnp.pad(to_res(t), ((0, 0), (0, 0), (0, 0), (side, side), (0, 0)))
        t = t.reshape(B, dil, H, nb + 2, side, Dh)
        return jnp.concatenate([t[:, :, :, :-2], t[:, :, :, 1:-1], t[:, :, :, 2:]], axis=4)

    qb = to_res(q).reshape(B, dil, H, nb, side, Dh)
    kb, vb = band(k), band(v)
    logits = jnp.einsum('brhnqd,brhnkd->brhnqk', qb, kb).astype(jnp.float32) * (Dh ** -0.5)

    t_q = jnp.arange(side)
    t_k = jnp.arange(3 * side)
    rel = t_k[None, :] - side - t_q[:, None]
    bias = rel_bias[t5_bucket(rel * dil)].transpose(2, 0, 1).astype(jnp.float32)
    kj = (jnp.arange(nb)[:, None] - 1) * side + t_k[None, :]
    valid = (jnp.abs(rel) <= side)[None] & ((kj >= 0) & (kj < L))[:, None, :]
    logits = jnp.where(valid[None, None, None], logits + bias[None, None, :, None], NEG_INF)

    lse = jax.nn.logsumexp(logits, axis=-1)
    p = jnp.exp(logits - lse[..., None]).astype(v.dtype)
    o = jnp.einsum('brhnqk,brhnkd->brhnqd', p, vb)
    o = o.reshape(B, dil, H, Lp, Dh)[:, :, :, :L].transpose(0, 3, 1, 2, 4).reshape(B, S, H, Dh)
    lse = lse.reshape(B, dil, H, Lp)[:, :, :, :L].transpose(0, 3, 1, 2).reshape(B, S, H)
    return o, lse


def dilated_attention(q, k, v, rel_bias):
    outs, lses = [], []
    for window, dil in DIL_PAIRS:
        o, lse = dilated_window_branch(q, k, v, rel_bias, window, dil)
        outs.append(o)
        lses.append(lse)
    w = jax.nn.softmax(jnp.stack(lses, axis=0), axis=0).astype(q.dtype)
    return jnp.einsum('gbsh,gbshd->bshd', w, jnp.stack(outs, axis=0))


def multiscale_pool(p, pool_w, pool_scale):
    B, S, C = p.shape
    G = len(POOL_WINDOWS)
    pg = p.reshape(B, S, G, POOL_GROUP).astype(jnp.float32)
    cs = jnp.pad(jnp.cumsum(pg, axis=1), ((0, 0), (1, 0), (0, 0), (0, 0)))
    pos = jnp.arange(S)
    means = []
    for gi, w in enumerate(POOL_WINDOWS):
        lo = jnp.clip(pos - w // 2, 0, S)
        hi = jnp.clip(pos + w // 2, 0, S)
        total = cs[:, hi, gi] - cs[:, lo, gi]
        means.append(total / (hi - lo).astype(jnp.float32)[None, :, None])
    pooled = (jnp.stack(means, axis=2) - pg).astype(p.dtype)
    mixed = jnp.einsum('bsgc,gcd->bsgd', pooled, pool_w)
    return mixed.reshape(B, S, C) * pool_scale


def peer_ffn(h, wq, subkeys, u_tab, v_tab):
    B, S, D = h.shape
    q = (h @ wq).reshape(B, S, PEER_HEADS, 2, PEER_QDIM // 2)
    s = jnp.einsum('bshpd,hpkd->bshpk', q, subkeys).astype(jnp.float32)
    s1, i1 = lax.top_k(s[..., 0, :], PEER_TOPK)
    s2, i2 = lax.top_k(s[..., 1, :], PEER_TOPK)
    cand = (s1[..., :, None] + s2[..., None, :]).reshape(B, S, PEER_HEADS, PEER_TOPK * PEER_TOPK)
    top, ci = lax.top_k(cand, PEER_TOPK)
    e1 = jnp.take_along_axis(i1, ci // PEER_TOPK, axis=-1)
    e2 = jnp.take_along_axis(i2, ci % PEER_TOPK, axis=-1)
    experts = e1 * N_KEYS + e2
    gates = jax.nn.softmax(top, axis=-1).astype(h.dtype)

    T = B * S
    HK = PEER_HEADS * PEER_TOPK
    nc = T // PEER_CHUNK

    def eval_chunk(args):
        hc, ec, gc = args
        a = jnp.einsum('cd,ckd->ck', hc, u_tab[ec])
        act = jax.nn.gelu(a) * gc
        return jnp.einsum('ck,ckd->cd', act, v_tab[ec])

    out = lax.map(eval_chunk, (h.reshape(nc, PEER_CHUNK, D),
                               experts.reshape(nc, PEER_CHUNK, HK),
                               gates.reshape(nc, PEER_CHUNK, HK)))
    return out.reshape(B, S, D)


def setup_inputs(seed: int = 0) -> dict:
    key = jax.random.key(seed)
    ks = jax.random.split(key, 16)
    nrm = jax.random.normal
    f32 = jnp.float32
    x = nrm(ks[0], (BATCH, SEQ, D_MODEL), f32)
    c = nrm(ks[1], (BATCH, D_MODEL), f32)
    w_ada = nrm(ks[2], (DEPTH, D_MODEL, 6 * D_MODEL), f32) * (0.5 * D_MODEL ** -0.5)
    b_ada = nrm(ks[3], (DEPTH, 6 * D_MODEL), f32) * 0.02
    g_mix = 1.0 + 0.05 * nrm(ks[4], (DEPTH, D_MODEL), f32)
    w_in = nrm(ks[5], (DEPTH, D_MODEL, IN_WIDTH), f32) * D_MODEL ** -0.5
    rel_bias = nrm(ks[6], (NUM_BUCKETS, ATT_HEADS), f32) * 0.5
    pool_w = nrm(ks[7], (DEPTH, len(POOL_WINDOWS), POOL_GROUP, POOL_GROUP), f32) * POOL_GROUP ** -0.5
    pool_scale = 1.0 + 0.1 * nrm(ks[8], (DEPTH, POOL_WIDTH), f32)
    w_out = nrm(ks[9], (DEPTH, MIX_WIDTH, D_MODEL), f32) * MIX_WIDTH ** -0.5
    g_ffn = 1.0 + 0.05 * nrm(ks[10], (DEPTH, D_MODEL), f32)
    peer_wq = nrm(ks[11], (DEPTH, D_MODEL, PEER_HEADS * PEER_QDIM), f32) * D_MODEL ** -0.5
    peer_subkeys = nrm(ks[12], (DEPTH, PEER_HEADS, 2, N_KEYS, PEER_QDIM // 2), f32) * (PEER_QDIM // 2) ** -0.5
    peer_u = nrm(ks[13], (DEPTH, N_EXPERTS, D_MODEL), f32) * D_MODEL ** -0.5
    peer_v = nrm(ks[14], (DEPTH, N_EXPERTS, D_MODEL), f32) * PEER_HEADS ** -0.5
    g_final = 1.0 + 0.05 * nrm(ks[15], (D_MODEL,), f32)
    return {"x": x, "c": c, "w_ada": w_ada, "b_ada": b_ada, "g_mix": g_mix, "w_in": w_in,
            "rel_bias": rel_bias, "pool_w": pool_w, "pool_scale": pool_scale, "w_out": w_out,
            "g_ffn": g_ffn, "peer_wq": peer_wq, "peer_subkeys": peer_subkeys,
            "peer_u": peer_u, "peer_v": peer_v, "g_final": g_final}


def reference(x, c, w_ada, b_ada, g_mix, w_in, rel_bias, pool_w, pool_scale, w_out,
              g_ffn, peer_wq, peer_subkeys, peer_u, peer_v, g_final):
    B, S, _ = x.shape
    for l in range(DEPTH):
        mod = jax.nn.silu(c) @ w_ada[l] + b_ada[l]
        sh1, sc1, gt1, sh2, sc2, gt2 = jnp.split(mod, 6, axis=-1)

        h = rmsnorm(x, g_mix[l]) * (1 + sc1[:, None]) + sh1[:, None]
        proj = h @ w_in[l]
        q = proj[..., :ATT_WIDTH].reshape(B, S, ATT_HEADS, HEAD_DIM)
        k = proj[..., ATT_WIDTH:2 * ATT_WIDTH].reshape(B, S, ATT_HEADS, HEAD_DIM)
        v = proj[..., 2 * ATT_WIDTH:3 * ATT_WIDTH].reshape(B, S, ATT_HEADS, HEAD_DIM)
        p_in = proj[..., 3 * ATT_WIDTH:]
        attn = dilated_attention(q, k, v, rel_bias).reshape(B, S, ATT_WIDTH)
        pooled = multiscale_pool(p_in, pool_w[l], pool_scale[l])
        mix = jnp.concatenate([attn, pooled], axis=-1) @ w_out[l]
        x = x + gt1[:, None] * mix

        h2 = rmsnorm(x, g_ffn[l]) * (1 + sc2[:, None]) + sh2[:, None]
        x = x + gt2[:, None] * peer_ffn(h2, peer_wq[l], peer_subkeys[l], peer_u[l], peer_v[l])
    return rmsnorm(x, g_final)
```

```python
import functools
import math

import numpy as np
import jax
import jax.numpy as jnp
from jax import lax
from jax.experimental import pallas as pl
from jax.experimental.pallas import tpu as pltpu

HEAD_DIM = 128
DIL_PAIRS = ((128, 1), (512, 4), (2048, 16))
NUM_BUCKETS = 32
MAX_DISTANCE = 1024
POOL_WINDOWS = (2, 4, 8, 16)
PEER_TOPK = 16
RMS_EPS = 1e-6
NEG_INF = -1e30

LANE = 128
SUBLANE = 8
VMEM_LIMIT_BYTES = 56 * 1024 * 1024

ATT_SIDE = 64
ATT_TQ = 2 * ATT_SIDE
ATT_TK = ATT_TQ + 2 * ATT_SIDE
ATT_MACRO = ATT_TQ * max(d for _, d in DIL_PAIRS)

F32 = jnp.float32
BF16 = jnp.bfloat16
NT_DIMS = (((1,), (1,)), ((), ()))


def _params(*semantics):
    return pltpu.CompilerParams(dimension_semantics=semantics, vmem_limit_bytes=VMEM_LIMIT_BYTES)


def _rows(start, size, stride):
    return pl.ds(start, size) if stride == 1 else pl.ds(start, size, stride=stride)


def _ada_kernel(c_ref, w_ref, b_ref, o_ref):
    s = jax.nn.silu(c_ref[...])
    o_ref[...] = jnp.dot(s, w_ref[...], precision=lax.Precision.HIGHEST,
                         preferred_element_type=F32) + b_ref[...]


def _ada(c_pad, w_ada, b_ada, layer, tn=512):
    rows, d = c_pad.shape
    n = w_ada.shape[-1]
    return pl.pallas_call(
        _ada_kernel,
        grid=(n // tn,),
        in_specs=[pl.BlockSpec((rows, d), lambda j: (0, 0)),
                  pl.BlockSpec((None, d, tn), lambda j: (layer, 0, j)),
                  pl.BlockSpec((None, 1, tn), lambda j: (layer, 0, j))],
        out_specs=pl.BlockSpec((rows, tn), lambda j: (0, j)),
        out_shape=jax.ShapeDtypeStruct((rows, n), F32),
        compiler_params=_params("arbitrary"),
        name="ada",
    )(c_pad, w_ada, b_ada.reshape(b_ada.shape[0], 1, n))


def _modulated_rmsnorm(x, gain, scale, shift):
    y = x * lax.rsqrt(jnp.mean(x * x, axis=-1, keepdims=True) + RMS_EPS)
    return (y * gain) * (1.0 + scale) + shift


def _inproj_kernel(x_ref, g_ref, sc_ref, sh_ref, w_ref, o_ref, h_scr):
    @pl.when(pl.program_id(1) == 0)
    def _():
        h_scr[...] = _modulated_rmsnorm(x_ref[...], g_ref[...], sc_ref[...], sh_ref[...]).astype(BF16)

    acc = jnp.dot(h_scr[...], w_ref[...], preferred_element_type=F32)
    for k in range(o_ref.shape[0]):
        o_ref[k] = acc[:, k * LANE:(k + 1) * LANE]


def _inproj(xt, gain, scale, shift, w_bf16, layer, seq, tm, tn):
    t, d = xt.shape
    n = w_bf16.shape[-1]
    per_batch = seq // tm
    return pl.pallas_call(
        _inproj_kernel,
        grid=(t // tm, n // tn),
        in_specs=[pl.BlockSpec((tm, d), lambda i, j: (i, 0)),
                  pl.BlockSpec((None, 1, d), lambda i, j: (layer, 0, 0)),
                  pl.BlockSpec((None, 1, d), lambda i, j: (i // per_batch, 0, 0)),
                  pl.BlockSpec((None, 1, d), lambda i, j: (i // per_batch, 0, 0)),
                  pl.BlockSpec((d, tn), lambda i, j: (0, j))],
        out_specs=pl.BlockSpec((tn // LANE, tm, LANE), lambda i, j: (j, i, 0)),
        out_shape=jax.ShapeDtypeStruct((n // LANE, t, LANE), F32),
        scratch_shapes=[pltpu.VMEM((tm, d), BF16)],
        compiler_params=_params("parallel", "arbitrary"),
        name="inproj",
    )(xt, gain.reshape(gain.shape[0], 1, d), scale, shift, w_bf16)


def _t5_bucket_np(rel):
    half = NUM_BUCKETS // 2
    n = -rel
    ret = np.where(n < 0, half, 0)
    n = np.abs(n)
    max_exact = half // 2
    nf = np.maximum(n, 1).astype(np.float32)
    large = max_exact + (np.log(nf / np.float32(max_exact)) / np.float32(math.log(MAX_DISTANCE / max_exact))
                         * np.float32(half - max_exact)).astype(np.int32)
    large = np.minimum(large, half - 1)
    return ret + np.where(n < max_exact, n, large)


def _attention_bias_tables(rel_bias):
    i = np.arange(ATT_TQ)[:, None]
    j = np.arange(ATT_TK)[None, :]
    tables = []
    for _, dil in DIL_PAIRS:
        per_variant = []
        for off in (0, -ATT_SIDE, -2 * ATT_SIDE):
            rel = j + off - i
            valid = np.abs(rel) <= ATT_SIDE
            bucket = _t5_bucket_np(rel * dil)
            vals = jnp.transpose(rel_bias[bucket], (2, 0, 1)).astype(F32)
            per_variant.append(jnp.where(valid[None], vals, NEG_INF))
        tables.append(jnp.stack(per_variant, axis=1))
    return jnp.stack(tables, axis=1)


def _attn_kernel(q_ref, k_ref, v_ref, bias_ref, o_ref, ob_scr, lse_scr, *, seq):
    scale = HEAD_DIM ** -0.5

    def macro_tile(t, carry):
        p0 = pl.multiple_of(t * ATT_MACRO, ATT_MACRO)
        for g, (_, dil) in enumerate(DIL_PAIRS):
            length = seq // dil
            nblk = ATT_MACRO // (dil * ATT_TQ)

            def block(it, c, g=g, dil=dil, length=length, nblk=nblk):
                r = it % dil
                n = it // dil
                q0 = p0 // dil + n * ATT_TQ
                k0 = jnp.clip(q0 - ATT_SIDE, 0, length - ATT_TK)
                variant = jnp.where(q0 == 0, 0, jnp.where(q0 == length - ATT_TQ, 2, 1))
                q = q_ref[_rows(r + q0 * dil, ATT_TQ, dil), :].astype(BF16)
                k = k_ref[_rows(r + k0 * dil, ATT_TK, dil), :].astype(BF16)
                v = v_ref[_rows(r + k0 * dil, ATT_TK, dil), :].astype(BF16)
                s = lax.dot_general(q, k, NT_DIMS, preferred_element_type=F32)
                s = s * scale + bias_ref[g, variant]
                m = jnp.max(s, axis=1, keepdims=True)
                p = jnp.exp(s - m)
                l = jnp.sum(p, axis=1, keepdims=True)
                o = jnp.dot(p.astype(BF16), v, preferred_element_type=F32) / l
                lse = m + jnp.log(l)
                dst = _rows(r + n * ATT_TQ * dil, ATT_TQ, dil)
                ob_scr[g, dst, :] = o
                lse_scr[g, dst, :] = jnp.broadcast_to(lse, (ATT_TQ, LANE))
                return c

            lax.fori_loop(0, dil * nblk, block, 0)

        chunk = 256

        def merge(ci, c):
            rows = pl.ds(pl.multiple_of(ci * chunk, chunk), chunk)
            l0, l1, l2 = lse_scr[0, rows, :], lse_scr[1, rows, :], lse_scr[2, rows, :]
            mx = jnp.maximum(jnp.maximum(l0, l1), l2)
            e0, e1, e2 = jnp.exp(l0 - mx), jnp.exp(l1 - mx), jnp.exp(l2 - mx)
            den = e0 + e1 + e2
            out = (e0 / den) * ob_scr[0, rows, :] + (e1 / den) * ob_scr[1, rows, :] + (e2 / den) * ob_scr[2, rows, :]
            o_ref[pl.ds(p0 + pl.multiple_of(ci * chunk, chunk), chunk), :] = out.astype(o_ref.dtype)
            return c

        lax.fori_loop(0, ATT_MACRO // chunk, merge, 0)
        return carry

    lax.fori_loop(0, seq // ATT_MACRO, macro_tile, 0)


def _attention(proj_cm, bias_tables, batch, seq, heads):
    t = proj_cm.shape[1]
    assert seq % ATT_MACRO == 0 and seq // max(d for _, d in DIL_PAIRS) >= ATT_TK
    nb = len(DIL_PAIRS)

    def qkv_spec(off):
        return pl.BlockSpec((None, seq, LANE), lambda b, h, off=off: (off + h, b, 0))

    return pl.pallas_call(
        functools.partial(_attn_kernel, seq=seq),
        grid=(batch, heads),
        in_specs=[qkv_spec(0), qkv_spec(heads), qkv_spec(2 * heads),
                  pl.BlockSpec((None, nb, 3, ATT_TQ, ATT_TK), lambda b, h: (h, 0, 0, 0, 0))],
        out_specs=pl.BlockSpec((None, seq, LANE), lambda b, h: (h, b, 0)),
        out_shape=jax.ShapeDtypeStruct((heads, t, LANE), BF16),
        scratch_shapes=[pltpu.VMEM((nb, ATT_MACRO, LANE), F32),
                        pltpu.VMEM((nb, ATT_MACRO, LANE), F32)],
        compiler_params=_params("parallel", "arbitrary"),
        name="attn",
    )(proj_cm, proj_cm, proj_cm, bias_tables)


def _outproj_kernel(attn_ref, p_ref, prev_ref, next_ref, pw_ref, ps_ref, w_ref, x_ref, gt_ref, o_ref,
                    lhs_scr, pooled_scr, ext_scr, *, seq, tm):
    i = pl.program_id(0)
    heads = attn_ref.shape[0]
    att_w = heads * LANE
    n_pool_blocks = p_ref.shape[0]
    pool_group = pw_ref.shape[-1]

    @pl.when(pl.program_id(1) == 0)
    def _():
        for h in range(heads):
            lhs_scr[:, h * LANE:(h + 1) * LANE] = attn_ref[h]

        base = (i * tm) % seq
        has_prev = (base != 0).astype(F32)
        has_next = (base + tm != seq).astype(F32)
        pos = base + lax.broadcasted_iota(jnp.int32, (tm, LANE), 0)
        for cb in range(n_pool_blocks):
            w = POOL_WINDOWS[(cb * LANE) // pool_group]
            main = p_ref[cb]
            ext_scr[0:SUBLANE, :] = prev_ref[cb] * has_prev
            ext_scr[SUBLANE:SUBLANE + tm, :] = main
            ext_scr[SUBLANE + tm:2 * SUBLANE + tm, :] = next_ref[cb] * has_next
            total = ext_scr[pl.ds(SUBLANE - w // 2, tm), :]
            for off in range(-w // 2 + 1, w // 2):
                total = total + ext_scr[pl.ds(SUBLANE + off, tm), :]
            count = (jnp.minimum(pos + w // 2, seq) - jnp.maximum(pos - w // 2, 0)).astype(F32)
            pooled_scr[:, cb * LANE:(cb + 1) * LANE] = (total / count - main).astype(BF16)
        for gi in range(len(POOL_WINDOWS)):
            cols = slice(gi * pool_group, (gi + 1) * pool_group)
            mixed = jnp.dot(pooled_scr[:, cols], pw_ref[gi], preferred_element_type=F32)
            lhs_scr[:, att_w + gi * pool_group:att_w + (gi + 1) * pool_group] = (mixed * ps_ref[:, cols]).astype(BF16)

    mix = jnp.dot(lhs_scr[...], w_ref[...], preferred_element_type=F32)
    o_ref[...] = x_ref[...] + gt_ref[...] * mix


def _outproj(attn_hm, proj_cm, pool_w_bf16, pool_scale, w_out_bf16, xt, gate, layer, seq, heads, tm, tn):
    t, d = xt.shape
    pool_w_total = pool_scale.shape[-1]
    npb = pool_w_total // LANE
    pool_blk = (3 * heads) // npb
    assert pool_blk * npb == 3 * heads and tm % SUBLANE == 0
    per_batch = seq // tm
    halo_per_tile = tm // SUBLANE
    n_halo = t // SUBLANE
    mix_w = w_out_bf16.shape[0]
    return pl.pallas_call(
        functools.partial(_outproj_kernel, seq=seq, tm=tm),
        grid=(t // tm, d // tn),
        in_specs=[pl.BlockSpec((heads, tm, LANE), lambda i, j: (0, i, 0)),
                  pl.BlockSpec((npb, tm, LANE), lambda i, j: (pool_blk, i, 0)),
                  pl.BlockSpec((npb, SUBLANE, LANE),
                               lambda i, j: (pool_blk, jnp.maximum(i * halo_per_tile - 1, 0), 0)),
                  pl.BlockSpec((npb, SUBLANE, LANE),
                               lambda i, j: (pool_blk, jnp.minimum((i + 1) * halo_per_tile, n_halo - 1), 0)),
                  pl.BlockSpec(pool_w_bf16.shape, lambda i, j: (0, 0, 0)),
                  pl.BlockSpec((None, 1, pool_w_total), lambda i, j: (layer, 0, 0)),
                  pl.BlockSpec((mix_w, tn), lambda i, j: (0, j)),
                  pl.BlockSpec((tm, tn), lambda i, j: (i, j)),
                  pl.BlockSpec((None, 1, tn), lambda i, j: (i // per_batch, 0, j))],
        out_specs=pl.BlockSpec((tm, tn), lambda i, j: (i, j)),
        out_shape=jax.ShapeDtypeStruct((t, d), F32),
        scratch_shapes=[pltpu.VMEM((tm, mix_w), BF16),
                        pltpu.VMEM((tm, pool_w_total), BF16),
                        pltpu.VMEM((tm + 2 * SUBLANE, LANE), F32)],
        compiler_params=_params("parallel", "arbitrary"),
        name="outproj",
    )(attn_hm, proj_cm, proj_cm, proj_cm, pool_w_bf16,
      pool_scale.reshape(pool_scale.shape[0], 1, pool_w_total), w_out_bf16, xt, gate)


def _peer_q_kernel(x_ref, g_ref, sc_ref, sh_ref, wq_ref, sk_ref, h_ref, s_ref):
    @pl.when(pl.program_id(1) == 0)
    def _():
        h_ref[...] = _modulated_rmsnorm(x_ref[...], g_ref[...], sc_ref[...], sh_ref[...]).astype(BF16)

    q = jnp.dot(h_ref[...], wq_ref[...], preferred_element_type=F32).astype(BF16)
    half = sk_ref.shape[-1]
    nk = sk_ref.shape[-2]
    for p in range(2):
        s_ref[p * nk:(p + 1) * nk, :] = lax.dot_general(
            sk_ref[p], q[:, p * half:(p + 1) * half], NT_DIMS, preferred_element_type=F32)


def _peer_q(xt, gain, scale, shift, wq_bf16, subkeys_bf16, layer, seq, tm):
    t, d = xt.shape
    ph, _, nk, half = subkeys_bf16.shape
    per_batch = seq // tm
    return pl.pallas_call(
        _peer_q_kernel,
        grid=(t // tm, ph),
        in_specs=[pl.BlockSpec((tm, d), lambda i, h: (i, 0)),
                  pl.BlockSpec((None, 1, d), lambda i, h: (layer, 0, 0)),
                  pl.BlockSpec((None, 1, d), lambda i, h: (i // per_batch, 0, 0)),
                  pl.BlockSpec((None, 1, d), lambda i, h: (i // per_batch, 0, 0)),
                  pl.BlockSpec((d, 2 * half), lambda i, h: (0, h)),
                  pl.BlockSpec((None, 2, nk, half), lambda i, h: (h, 0, 0, 0))],
        out_specs=[pl.BlockSpec((tm, d), lambda i, h: (i, 0)),
                   pl.BlockSpec((2 * nk, tm), lambda i, h: (h, i))],
        out_shape=[jax.ShapeDtypeStruct((t, d), BF16),
                   jax.ShapeDtypeStruct((ph * 2 * nk, t), F32)],
        compiler_params=_params("parallel", "arbitrary"),
        name="peer_q",
    )(xt, gain.reshape(gain.shape[0], 1, d), scale, shift, wq_bf16, subkeys_bf16)


_CANDS = tuple((a, b) for a in range(PEER_TOPK) for b in range(PEER_TOPK) if (a + 1) * (b + 1) <= PEER_TOPK)
TOPK_TOKENS = SUBLANE * LANE


def _topk_kernel(s_ref, c1_ref, p1_ref, r2_ref, p2_ref, cur_scr, rank_scr, top_scr, cand_scr, cnt_scr):
    nk = s_ref.shape[0] // 2
    vshape = s_ref.shape[1:]
    neg = jnp.full(vshape, -jnp.inf, F32)

    def extract_first(ref, n, k_val, on_hit):
        m = lax.fori_loop(1, n, lambda i, m: jnp.maximum(m, ref[i]), ref[0], unroll=8)

        def find(i, idx):
            key = n - 1 - i
            return jnp.where(ref[key] == m, key.astype(F32), idx)

        idx = lax.fori_loop(0, n, find, jnp.full(vshape, float(n), F32), unroll=8)

        def mark(key, c):
            hit = idx == key.astype(F32)
            ref[key] = jnp.where(hit, neg, ref[key])
            on_hit(key, hit, k_val)
            return c

        lax.fori_loop(0, n, mark, 0, unroll=8)
        return m

    for half in range(2):
        def init(key, c, half=half):
            cur_scr[key] = s_ref[half * nk + key]
            rank_scr[half * nk + key] = jnp.full(vshape, float(PEER_TOPK), F32)
            return c

        lax.fori_loop(0, nk, init, 0, unroll=8)

        def on_hit(key, hit, k_val, half=half):
            rank_scr[half * nk + key] = jnp.where(hit, k_val, rank_scr[half * nk + key])

        def step(k, c, half=half, on_hit=on_hit):
            top_scr[half * PEER_TOPK + k] = extract_first(cur_scr, nk, k.astype(F32), on_hit)
            return c

        lax.fori_loop(0, PEER_TOPK, step, 0)

    for ci, (a, b) in enumerate(_CANDS):
        cand_scr[ci] = top_scr[a] + top_scr[PEER_TOPK + b]
    for a in range(PEER_TOPK):
        cnt_scr[a] = jnp.zeros(vshape, F32)
    best = top_scr[0] + top_scr[PEER_TOPK]

    def on_cand_hit(ci, hit, _):
        a = _cand_first_rank(ci)
        cnt_scr[a] = cnt_scr[a] + jnp.where(hit, 1.0, 0.0)

    def cand_step(k, z):
        m = extract_first(cand_scr, len(_CANDS), None, on_cand_hit)
        return z + jnp.exp(m - best)

    z = lax.fori_loop(0, PEER_TOPK, cand_step, jnp.zeros(vshape, F32))
    zinv = 1.0 / z
    top1, top2 = top_scr[0], top_scr[PEER_TOPK]

    def emit(key, c):
        r1 = rank_scr[key]
        cnt = jnp.zeros(vshape, F32)
        for a in range(PEER_TOPK):
            cnt = jnp.where(r1 == float(a), cnt_scr[a], cnt)
        c1_ref[key] = cnt
        p1_ref[key] = jnp.exp(s_ref[key] - top1) * zinv
        r2_ref[key] = rank_scr[nk + key]
        p2_ref[key] = jnp.exp(s_ref[nk + key] - top2)
        return c

    lax.fori_loop(0, nk, emit, 0, unroll=2)


_CAND_STARTS = tuple(int(np.searchsorted([a for a, _ in _CANDS], a)) for a in range(PEER_TOPK))


def _cand_first_rank(ci):
    a = jnp.int32(0)
    for start in _CAND_STARTS[1:]:
        a = a + (ci >= start).astype(jnp.int32)
    return a


def _peer_topk(s_t, ph, nk):
    t = s_t.shape[1]
    s3 = s_t.reshape(ph * 2 * nk, t // LANE, LANE)
    out = jax.ShapeDtypeStruct((ph * nk, t // LANE, LANE), F32)
    slab = (SUBLANE, LANE)
    out_spec = pl.BlockSpec((nk,) + slab, lambda h, i: (h, i, 0))
    res = pl.pallas_call(
        _topk_kernel,
        grid=(ph, t // TOPK_TOKENS),
        in_specs=[pl.BlockSpec((2 * nk,) + slab, lambda h, i: (h, i, 0))],
        out_specs=[out_spec] * 4,
        out_shape=[out] * 4,
        scratch_shapes=[pltpu.VMEM((nk,) + slab, F32),
                        pltpu.VMEM((2 * nk,) + slab, F32),
                        pltpu.VMEM((2 * PEER_TOPK,) + slab, F32),
                        pltpu.VMEM((len(_CANDS),) + slab, F32),
                        pltpu.VMEM((PEER_TOPK,) + slab, F32)],
        compiler_params=_params("parallel", "parallel"),
        name="peer_topk",
    )(s3)
    return [r.reshape(ph * nk, t) for r in res]


def _peer_ffn_kernel(h_ref, u_ref, vt_ref, c1_ref, p1_ref, r2_ref, p2_ref, o_ref, *, heads, nk):
    j = pl.program_id(1)
    te = u_ref.shape[0]

    @pl.when(j == 0)
    def _():
        o_ref[...] = jnp.zeros_like(o_ref)

    a = lax.dot_general(u_ref[...], h_ref[...], NT_DIMS, preferred_element_type=F32)
    act = jax.nn.gelu(a)
    parts = []
    for e in range(te // nk):
        e1 = j * (te // nk) + e
        gate = jnp.zeros((nk, a.shape[1]), F32)
        for h in range(heads):
            c1 = c1_ref[pl.ds(h * nk + e1, 1), :]
            p1 = p1_ref[pl.ds(h * nk + e1, 1), :]
            r2 = r2_ref[h * nk:(h + 1) * nk, :]
            p2 = p2_ref[h * nk:(h + 1) * nk, :]
            gate = gate + jnp.where(r2 < c1, p1 * p2, 0.0)
        parts.append((act[e * nk:(e + 1) * nk] * gate).astype(BF16))
    act_bf16 = parts[0] if len(parts) == 1 else jnp.concatenate(parts, axis=0)
    o_ref[...] += jnp.dot(vt_ref[...], act_bf16, preferred_element_type=F32)


def _peer_ffn(h2, u_bf16, vt_bf16, tables, heads, nk, tt, te):
    t, d = h2.shape
    n_exp = u_bf16.shape[0]
    tab_spec = pl.BlockSpec((heads * nk, tt), lambda i, j: (0, i))
    return pl.pallas_call(
        functools.partial(_peer_ffn_kernel, heads=heads, nk=nk),
        grid=(t // tt, n_exp // te),
        in_specs=[pl.BlockSpec((tt, d), lambda i, j: (i, 0)),
                  pl.BlockSpec((te, d), lambda i, j: (j, 0)),
                  pl.BlockSpec((d, te), lambda i, j: (0, j))] + [tab_spec] * 4,
        out_specs=pl.BlockSpec((d, tt), lambda i, j: (0, i)),
        out_shape=jax.ShapeDtypeStruct((d, t), F32),
        compiler_params=_params("parallel", "arbitrary"),
        name="peer_ffn",
    )(h2, u_bf16, vt_bf16, *tables)


def _residual_kernel(x_ref, pt_ref, gt_ref, g_ref, o_ref, *, final_norm):
    x = x_ref[...] + gt_ref[...] * jnp.transpose(pt_ref[...])
    if final_norm:
        x = (x * lax.rsqrt(jnp.mean(x * x, axis=-1, keepdims=True) + RMS_EPS)) * g_ref[...]
    o_ref[...] = x


def _residual(xt, peer_t, gate, g_final, seq, tm, final_norm):
    t, d = xt.shape
    per_batch = seq // tm
    return pl.pallas_call(
        functools.partial(_residual_kernel, final_norm=final_norm),
        grid=(t // tm,),
        in_specs=[pl.BlockSpec((tm, d), lambda i: (i, 0)),
                  pl.BlockSpec((d, tm), lambda i: (0, i)),
                  pl.BlockSpec((None, 1, d), lambda i: (i // per_batch, 0, 0)),
                  pl.BlockSpec((1, d), lambda i: (0, 0))],
        out_specs=pl.BlockSpec((tm, d), lambda i: (i, 0)),
        out_shape=jax.ShapeDtypeStruct((t, d), F32),
        compiler_params=_params("parallel"),
        name="residual",
    )(xt, peer_t, gate, g_final.reshape(1, d))


def kernel(x, c, w_ada, b_ada, g_mix, w_in, rel_bias, pool_w, pool_scale, w_out, g_ffn, peer_wq, peer_subkeys,
           peer_u, peer_v, g_final):
    batch, seq, d = x.shape
    t = batch * seq
    depth = w_ada.shape[0]
    pool_total = pool_scale.shape[-1]
    att_w = (w_in.shape[-1] - pool_total) // 3
    heads = att_w // HEAD_DIM
    ph, _, nk, _ = peer_subkeys.shape[1:]
    tm = min(512, seq)
    tn = min(512, d)

    xt = x.reshape(t, d)
    c_pad = jnp.pad(c, ((0, SUBLANE - batch % SUBLANE if batch % SUBLANE else 0), (0, 0)))
    bias_tables = _attention_bias_tables(rel_bias)

    for layer in range(depth):
        mod = _ada(c_pad, w_ada, b_ada, layer, tn=min(512, d))[:batch]
        sh1, sc1, gt1, sh2, sc2, gt2 = [m.reshape(batch, 1, d) for m in jnp.split(mod, 6, axis=-1)]

        proj_cm = _inproj(xt, g_mix, sc1, sh1, w_in[layer].astype(BF16), layer, seq, tm, tn)
        attn_hm = _attention(proj_cm, bias_tables, batch, seq, heads)
        xt = _outproj(attn_hm, proj_cm, pool_w[layer].astype(BF16), pool_scale, w_out[layer].astype(BF16),
                      xt, gt1, layer, seq, heads, tm, tn)

        h2, s_t = _peer_q(xt, g_ffn, sc2, sh2, peer_wq[layer].astype(BF16),
                          peer_subkeys[layer].astype(BF16), layer, seq, tm)
        tables = _peer_topk(s_t, ph, nk)
        peer_t = _peer_ffn(h2, peer_u[layer].astype(BF16), jnp.transpose(peer_v[layer]).astype(BF16),
                           tables, ph, nk, tt=min(512, t), te=2 * nk)
        xt = _residual(xt, peer_t, gt2, g_final, seq, min(256, seq), final_norm=(layer == depth - 1))

    return xt.reshape(batch, seq, d)
```

```python
import functools
import math

import numpy as np
import jax
import jax.numpy as jnp
from jax import lax
from jax.experimental import pallas as pl
from jax.experimental.pallas import tpu as pltpu

HEAD_DIM = 128
DIL_PAIRS = ((128, 1), (512, 4), (2048, 16))
NUM_BUCKETS = 32
MAX_DISTANCE = 1024
POOL_WINDOWS = (2, 4, 8, 16)
PEER_TOPK = 16
RMS_EPS = 1e-6
NEG_INF = -1e30

LANE = 128
SUBLANE = 8
VMEM_LIMIT_BYTES = 56 * 1024 * 1024

ATT_SIDE = 64
ATT_TQ = 2 * ATT_SIDE
ATT_TK = ATT_TQ + 2 * ATT_SIDE
ATT_MACRO = ATT_TQ * max(d for _, d in DIL_PAIRS)
ATT_VARIANTS = 3

F32 = jnp.float32
BF16 = jnp.bfloat16
NT_DIMS = (((1,), (1,)), ((), ()))


def _params(*semantics):
    return pltpu.CompilerParams(dimension_semantics=semantics, vmem_limit_bytes=VMEM_LIMIT_BYTES)


def _rows(start, size, stride):
    return pl.ds(start, size) if stride == 1 else pl.ds(start, size, stride=stride)


def _ada_kernel(c_ref, w_ref, b_ref, o_ref):
    s = jax.nn.silu(c_ref[...])
    o_ref[...] = jnp.dot(s, w_ref[...], precision=lax.Precision.HIGHEST,
                         preferred_element_type=F32) + b_ref[...]


def _ada(c_pad, w_ada, b_ada, layer, tn=512):
    rows, d = c_pad.shape
    n = w_ada.shape[-1]
    return pl.pallas_call(
        _ada_kernel,
        grid=(n // tn,),
        in_specs=[pl.BlockSpec((rows, d), lambda j: (0, 0)),
                  pl.BlockSpec((None, d, tn), lambda j: (layer, 0, j)),
                  pl.BlockSpec((None, 1, tn), lambda j: (layer, 0, j))],
        out_specs=pl.BlockSpec((rows, tn), lambda j: (0, j)),
        out_shape=jax.ShapeDtypeStruct((rows, n), F32),
        compiler_params=_params("arbitrary"),
        name="ada",
    )(c_pad, w_ada, b_ada.reshape(b_ada.shape[0], 1, n))


def _modulated_rmsnorm(x, gain, scale, shift):
    y = x * lax.rsqrt(jnp.mean(x * x, axis=-1, keepdims=True) + RMS_EPS)
    return (y * gain) * (1.0 + scale) + shift


def _inproj_kernel(x_ref, g_ref, sc_ref, sh_ref, w_ref, o_ref, h_scr):
    @pl.when(pl.program_id(1) == 0)
    def _():
        h_scr[...] = _modulated_rmsnorm(x_ref[...], g_ref[...], sc_ref[...], sh_ref[...]).astype(BF16)

    acc = jnp.dot(h_scr[...], w_ref[...], preferred_element_type=F32)
    for k in range(o_ref.shape[0]):
        o_ref[k] = acc[:, k * LANE:(k + 1) * LANE]


def _inproj(xt, gain, scale, shift, w_bf16, layer, seq, tm, tn):
    t, d = xt.shape
    n = w_bf16.shape[-1]
    per_batch = seq // tm
    return pl.pallas_call(
        _inproj_kernel,
        grid=(t // tm, n // tn),
        in_specs=[pl.BlockSpec((tm, d), lambda i, j: (i, 0)),
                  pl.BlockSpec((None, 1, d), lambda i, j: (layer, 0, 0)),
                  pl.BlockSpec((None, 1, d), lambda i, j: (i // per_batch, 0, 0)),
                  pl.BlockSpec((None, 1, d), lambda i, j: (i // per_batch, 0, 0)),
                  pl.BlockSpec((d, tn), lambda i, j: (0, j))],
        out_specs=pl.BlockSpec((tn // LANE, tm, LANE), lambda i, j: (j, i, 0)),
        out_shape=jax.ShapeDtypeStruct((n // LANE, t, LANE), F32),
        scratch_shapes=[pltpu.VMEM((tm, d), BF16)],
        compiler_params=_params("parallel", "arbitrary"),
        name="inproj",
    )(xt, gain.reshape(gain.shape[0], 1, d), scale, shift, w_bf16)


def _t5_bucket_np(rel):
    half = NUM_BUCKETS // 2
    n = -rel
    ret = np.where(n < 0, half, 0)
    n = np.abs(n)
    max_exact = half // 2
    nf = np.maximum(n, 1).astype(np.float32)
    large = max_exact + (np.log(nf / np.float32(max_exact)) / np.float32(math.log(MAX_DISTANCE / max_exact))
                         * np.float32(half - max_exact)).astype(np.int32)
    large = np.minimum(large, half - 1)
    return ret + np.where(n < max_exact, n, large)


def _attention_bucket_tables():
    i = np.arange(ATT_TQ)[:, None]
    j = np.arange(ATT_TK)[None, :]
    tables = []
    for _, dil in DIL_PAIRS:
        for off in (0, -ATT_SIDE, -2 * ATT_SIDE):
            rel = j + off - i
            tables.append(np.where(np.abs(rel) <= ATT_SIDE, _t5_bucket_np(rel * dil), -1))
    return np.stack(tables).astype(np.int32)


def _bias_kernel(rb_ref, bucket_ref, o_ref):
    h = pl.program_id(0)
    bucket = bucket_ref[...]
    out = jnp.full(bucket.shape, NEG_INF, F32)
    for b in range(NUM_BUCKETS):
        out = jnp.where(bucket == b, rb_ref[b, h], out)
    o_ref[...] = out


def _attention_bias_tables(rel_bias):
    buckets = jnp.asarray(_attention_bucket_tables())
    heads = rel_bias.shape[1]
    return pl.pallas_call(
        _bias_kernel,
        grid=(heads,),
        in_specs=[pl.BlockSpec(memory_space=pltpu.SMEM),
                  pl.BlockSpec(buckets.shape, lambda h: (0, 0, 0))],
        out_specs=pl.BlockSpec((None,) + buckets.shape, lambda h: (h, 0, 0, 0)),
        out_shape=jax.ShapeDtypeStruct((heads,) + buckets.shape, F32),
        compiler_params=_params("arbitrary"),
        name="attn_bias",
    )(rel_bias, buckets)


def _attn_kernel(q_ref, k_ref, v_ref, bias_ref, o_ref, ob_scr, lse_scr, *, seq):
    scale = HEAD_DIM ** -0.5

    def macro_tile(t, carry):
        p0 = pl.multiple_of(t * ATT_MACRO, ATT_MACRO)
        for g, (_, dil) in enumerate(DIL_PAIRS):
            length = seq // dil
            nblk = ATT_MACRO // (dil * ATT_TQ)

            def block(it, c, g=g, dil=dil, length=length, nblk=nblk):
                r = it % dil
                n = it // dil
                q0 = p0 // dil + n * ATT_TQ
                k0 = jnp.clip(q0 - ATT_SIDE, 0, length - ATT_TK)
                variant = jnp.where(q0 == 0, 0, jnp.where(q0 == length - ATT_TQ, 2, 1))
                q = q_ref[_rows(r + q0 * dil, ATT_TQ, dil), :].astype(BF16)
                k = k_ref[_rows(r + k0 * dil, ATT_TK, dil), :].astype(BF16)
                v = v_ref[_rows(r + k0 * dil, ATT_TK, dil), :].astype(BF16)
                s = lax.dot_general(q, k, NT_DIMS, preferred_element_type=F32)
                s = s * scale + bias_ref[g * ATT_VARIANTS + variant]
                m = jnp.max(s, axis=1, keepdims=True)
                p = jnp.exp(s - m)
                l = jnp.sum(p, axis=1, keepdims=True)
                o = jnp.dot(p.astype(BF16), v, preferred_element_type=F32) / l
                lse = m + jnp.log(l)
                dst = _rows(r + n * ATT_TQ * dil, ATT_TQ, dil)
                ob_scr[g, dst, :] = o
                lse_scr[g, dst, :] = jnp.broadcast_to(lse, (ATT_TQ, LANE))
                return c

            lax.fori_loop(0, dil * nblk, block, 0)

        chunk = 256

        def merge(ci, c):
            rows = pl.ds(pl.multiple_of(ci * chunk, chunk), chunk)
            l0, l1, l2 = lse_scr[0, rows, :], lse_scr[1, rows, :], lse_scr[2, rows, :]
            mx = jnp.maximum(jnp.maximum(l0, l1), l2)
            e0, e1, e2 = jnp.exp(l0 - mx), jnp.exp(l1 - mx), jnp.exp(l2 - mx)
            den = e0 + e1 + e2
            out = (e0 / den) * ob_scr[0, rows, :] + (e1 / den) * ob_scr[1, rows, :] + (e2 / den) * ob_scr[2, rows, :]
            o_ref[pl.ds(p0 + pl.multiple_of(ci * chunk, chunk), chunk), :] = out.astype(o_ref.dtype)
            return c

        lax.fori_loop(0, ATT_MACRO // chunk, merge, 0)
        return carry

    lax.fori_loop(0, seq // ATT_MACRO, macro_tile, 0)


def _attention(proj_cm, bias_tables, batch, seq, heads):
    t = proj_cm.shape[1]
    assert seq % ATT_MACRO == 0 and seq // max(d for _, d in DIL_PAIRS) >= ATT_TK
    nb = len(DIL_PAIRS)

    def qkv_spec(off):
        return pl.BlockSpec((None, seq, LANE), lambda b, h, off=off: (off + h, b, 0))

    return pl.pallas_call(
        functools.partial(_attn_kernel, seq=seq),
        grid=(batch, heads),
        in_specs=[qkv_spec(0), qkv_spec(heads), qkv_spec(2 * heads),
                  pl.BlockSpec((None, nb * ATT_VARIANTS, ATT_TQ, ATT_TK), lambda b, h: (h, 0, 0, 0))],
        out_specs=pl.BlockSpec((None, seq, LANE), lambda b, h: (h, b, 0)),
        out_shape=jax.ShapeDtypeStruct((heads, t, LANE), BF16),
        scratch_shapes=[pltpu.VMEM((nb, ATT_MACRO, LANE), F32),
                        pltpu.VMEM((nb, ATT_MACRO, LANE), F32)],
        compiler_params=_params("parallel", "arbitrary"),
        name="attn",
    )(proj_cm, proj_cm, proj_cm, bias_tables)


def _outproj_kernel(attn_ref, p_ref, prev_ref, next_ref, pw_ref, ps_ref, w_ref, x_ref, gt_ref, o_ref,
                    lhs_scr, pooled_scr, ext_scr, *, seq, tm):
    i = pl.program_id(0)
    heads = attn_ref.shape[0]
    att_w = heads * LANE
    n_pool_blocks = p_ref.shape[0]
    pool_group = pw_ref.shape[-1]

    @pl.when(pl.program_id(1) == 0)
    def _():
        for h in range(heads):
            lhs_scr[:, h * LANE:(h + 1) * LANE] = attn_ref[h]

        base = (i * tm) % seq
        has_prev = (base != 0).astype(F32)
        has_next = (base + tm != seq).astype(F32)
        pos = base + lax.broadcasted_iota(jnp.int32, (tm, LANE), 0)
        for cb in range(n_pool_blocks):
            w = POOL_WINDOWS[(cb * LANE) // pool_group]
            main = p_ref[cb]
            ext_scr[0:SUBLANE, :] = prev_ref[cb] * has_prev
            ext_scr[SUBLANE:SUBLANE + tm, :] = main
            ext_scr[SUBLANE + tm:2 * SUBLANE + tm, :] = next_ref[cb] * has_next
            total = ext_scr[pl.ds(SUBLANE - w // 2, tm), :]
            for off in range(-w // 2 + 1, w // 2):
                total = total + ext_scr[pl.ds(SUBLANE + off, tm), :]
            count = (jnp.minimum(pos + w // 2, seq) - jnp.maximum(pos - w // 2, 0)).astype(F32)
            pooled_scr[:, cb * LANE:(cb + 1) * LANE] = (total / count - main).astype(BF16)
        for gi in range(len(POOL_WINDOWS)):
            cols = slice(gi * pool_group, (gi + 1) * pool_group)
            mixed = jnp.dot(pooled_scr[:, cols], pw_ref[gi], preferred_element_type=F32)
            lhs_scr[:, att_w + gi * pool_group:att_w + (gi + 1) * pool_group] = (mixed * ps_ref[:, cols]).astype(BF16)

    mix = jnp.dot(lhs_scr[...], w_ref[...], preferred_element_type=F32)
    o_ref[...] = x_ref[...] + gt_ref[...] * mix


def _outproj(attn_hm, proj_cm, pool_w_bf16, pool_scale, w_out_bf16, xt, gate, layer, seq, heads, tm, tn):
    t, d = xt.shape
    pool_w_total = pool_scale.shape[-1]
    npb = pool_w_total // LANE
    pool_blk = (3 * heads) // npb
    assert pool_blk * npb == 3 * heads and tm % SUBLANE == 0
    per_batch = seq // tm
    halo_per_tile = tm // SUBLANE
    n_halo = t // SUBLANE
    mix_w = w_out_bf16.shape[0]
    return pl.pallas_call(
        functools.partial(_outproj_kernel, seq=seq, tm=tm),
        grid=(t // tm, d // tn),
        in_specs=[pl.BlockSpec((heads, tm, LANE), lambda i, j: (0, i, 0)),
                  pl.BlockSpec((npb, tm, LANE), lambda i, j: (pool_blk, i, 0)),
                  pl.BlockSpec((npb, SUBLANE, LANE),
                               lambda i, j: (pool_blk, jnp.maximum(i * halo_per_tile - 1, 0), 0)),
                  pl.BlockSpec((npb, SUBLANE, LANE),
                               lambda i, j: (pool_blk, jnp.minimum((i + 1) * halo_per_tile, n_halo - 1), 0)),
                  pl.BlockSpec(pool_w_bf16.shape, lambda i, j: (0, 0, 0)),
                  pl.BlockSpec((None, 1, pool_w_total), lambda i, j: (layer, 0, 0)),
                  pl.BlockSpec((mix_w, tn), lambda i, j: (0, j)),
                  pl.BlockSpec((tm, tn), lambda i, j: (i, j)),
                  pl.BlockSpec((None, 1, tn), lambda i, j: (i // per_batch, 0, j))],
        out_specs=pl.BlockSpec((tm, tn), lambda i, j: (i, j)),
        out_shape=jax.ShapeDtypeStruct((t, d), F32),
        scratch_shapes=[pltpu.VMEM((tm, mix_w), BF16),
                        pltpu.VMEM((tm, pool_w_total), BF16),
                        pltpu.VMEM((tm + 2 * SUBLANE, LANE), F32)],
        compiler_params=_params("parallel", "arbitrary"),
        name="outproj",
    )(attn_hm, proj_cm, proj_cm, proj_cm, pool_w_bf16,
      pool_scale.reshape(pool_scale.shape[0], 1, pool_w_total), w_out_bf16, xt, gate)


def _peer_q_kernel(x_ref, g_ref, sc_ref, sh_ref, wq_ref, sk_ref, h_ref, s_ref):
    @pl.when(pl.program_id(1) == 0)
    def _():
        h_ref[...] = _modulated_rmsnorm(x_ref[...], g_ref[...], sc_ref[...], sh_ref[...]).astype(BF16)

    q = jnp.dot(h_ref[...], wq_ref[...], preferred_element_type=F32).astype(BF16)
    half = sk_ref.shape[-1]
    nk = sk_ref.shape[-2]
    for p in range(2):
        s_ref[p * nk:(p + 1) * nk, :] = lax.dot_general(
            sk_ref[p], q[:, p * half:(p + 1) * half], NT_DIMS, preferred_element_type=F32)


def _peer_q(xt, gain, scale, shift, wq_bf16, subkeys_bf16, layer, seq, tm):
    t, d = xt.shape
    ph, _, nk, half = subkeys_bf16.shape
    per_batch = seq // tm
    return pl.pallas_call(
        _peer_q_kernel,
        grid=(t // tm, ph),
        in_specs=[pl.BlockSpec((tm, d), lambda i, h: (i, 0)),
                  pl.BlockSpec((None, 1, d), lambda i, h: (layer, 0, 0)),
                  pl.BlockSpec((None, 1, d), lambda i, h: (i // per_batch, 0, 0)),
                  pl.BlockSpec((None, 1, d), lambda i, h: (i // per_batch, 0, 0)),
                  pl.BlockSpec((d, 2 * half), lambda i, h: (0, h)),
                  pl.BlockSpec((None, 2, nk, half), lambda i, h: (h, 0, 0, 0))],
        out_specs=[pl.BlockSpec((tm, d), lambda i, h: (i, 0)),
                   pl.BlockSpec((2 * nk, tm), lambda i, h: (h, i))],
        out_shape=[jax.ShapeDtypeStruct((t, d), BF16),
                   jax.ShapeDtypeStruct((ph * 2 * nk, t), F32)],
        compiler_params=_params("parallel", "arbitrary"),
        name="peer_q",
    )(xt, gain.reshape(gain.shape[0], 1, d), scale, shift, wq_bf16, subkeys_bf16)


def _ranked_top(v, index):
    rank = jnp.full(v.shape, float(PEER_TOPK), F32)
    tops = []
    for k in range(PEER_TOPK):
        m = jnp.max(v, axis=0, keepdims=True)
        first = jnp.min(jnp.where(v == m, index, float(v.shape[0])), axis=0, keepdims=True)
        hit = index == first
        v = jnp.where(hit, -jnp.inf, v)
        rank = jnp.where(hit, float(k), rank)
        tops.append(m)
    return rank, tops


def _pair_counts(top1, top2):
    lanes = top1[0].shape[1]
    t1 = jnp.concatenate(top1, axis=0)
    a_iota = lax.broadcasted_iota(jnp.int32, (PEER_TOPK, lanes), 0).astype(F32)
    vals, idxs = [], []
    for b in range(PEER_TOPK):
        n_a = PEER_TOPK // (b + 1)
        rows = PEER_TOPK if n_a > SUBLANE else SUBLANE
        val = t1[:rows] + top2[b]
        if n_a < rows:
            val = jnp.where(a_iota[:rows] < float(n_a), val, -jnp.inf)
        vals.append(val)
        idxs.append(a_iota[:rows] * float(PEER_TOPK) + float(b))
    cand = jnp.concatenate(vals, axis=0)
    index = jnp.concatenate(idxs, axis=0)
    hits = jnp.zeros(cand.shape, F32)
    best = top1[0] + top2[0]
    z = jnp.zeros((1, lanes), F32)
    for _ in range(PEER_TOPK):
        m = jnp.max(cand, axis=0, keepdims=True)
        first = jnp.min(jnp.where(cand == m, index, float(PEER_TOPK * PEER_TOPK)), axis=0, keepdims=True)
        hit = index == first
        cand = jnp.where(hit, -jnp.inf, cand)
        hits = hits + jnp.where(hit, 1.0, 0.0)
        z = z + jnp.exp(m - best)
    cnt_lo = hits[0:SUBLANE]
    for b in range(1, PEER_TOPK):
        start = PEER_TOPK + (b - 1) * SUBLANE
        cnt_lo = cnt_lo + hits[start:start + SUBLANE]
    return jnp.concatenate([cnt_lo, hits[SUBLANE:PEER_TOPK]], axis=0), z


def _topk_kernel(s_ref, c1_ref, p1_ref, r2_ref, p2_ref):
    nk = s_ref.shape[0] // 2
    index = lax.broadcasted_iota(jnp.int32, (nk, LANE), 0).astype(F32)
    for strip in range(s_ref.shape[1] // LANE):
        cols = slice(strip * LANE, (strip + 1) * LANE)
        s1 = s_ref[0:nk, cols]
        s2 = s_ref[nk:2 * nk, cols]
        rank1, top1 = _ranked_top(s1, index)
        rank2, top2 = _ranked_top(s2, index)
        cnt, z = _pair_counts(top1, top2)
        c1 = jnp.zeros(s1.shape, F32)
        for a in range(PEER_TOPK):
            c1 = jnp.where(rank1 == float(a), cnt[a:a + 1], c1)
        c1_ref[:, cols] = c1
        p1_ref[:, cols] = jnp.exp(s1 - top1[0]) * (1.0 / z)
        r2_ref[:, cols] = rank2
        p2_ref[:, cols] = jnp.exp(s2 - top2[0])


def _peer_topk(s_t, ph, nk, tk):
    t = s_t.shape[1]
    out = jax.ShapeDtypeStruct((ph * nk, t), F32)
    out_spec = pl.BlockSpec((nk, tk), lambda h, i: (h, i))
    return pl.pallas_call(
        _topk_kernel,
        grid=(ph, t // tk),
        in_specs=[pl.BlockSpec((2 * nk, tk), lambda h, i: (h, i))],
        out_specs=[out_spec] * 4,
        out_shape=[out] * 4,
        compiler_params=_params("parallel", "parallel"),
        name="peer_topk",
    )(s_t)


def _peer_ffn_kernel(h_ref, u_ref, vt_ref, c1_ref, p1_ref, r2_ref, p2_ref, o_ref, *, nk):
    j = pl.program_id(1)
    te = u_ref.shape[0]
    heads = c1_ref.shape[0]
    keys_per_step = te // nk
    steps_per_tile = SUBLANE // keys_per_step

    @pl.when(j == 0)
    def _():
        o_ref[...] = jnp.zeros_like(o_ref)

    a = lax.dot_general(u_ref[...], h_ref[...], NT_DIMS, preferred_element_type=F32)
    act = jax.nn.gelu(a)
    parts = []
    for e in range(keys_per_step):
        row = (j % steps_per_tile) * keys_per_step + e
        gate = jnp.zeros((nk, a.shape[1]), F32)
        for h in range(heads):
            c1 = c1_ref[h, pl.ds(row, 1), :]
            p1 = p1_ref[h, pl.ds(row, 1), :]
            r2 = r2_ref[h * nk:(h + 1) * nk, :]
            p2 = p2_ref[h * nk:(h + 1) * nk, :]
            gate = gate + jnp.where(r2 < c1, p1 * p2, 0.0)
        parts.append((act[e * nk:(e + 1) * nk] * gate).astype(BF16))
    act_bf16 = parts[0] if len(parts) == 1 else jnp.concatenate(parts, axis=0)
    o_ref[...] += jnp.dot(vt_ref[...], act_bf16, preferred_element_type=F32)


def _peer_ffn(h2, u_bf16, vt_blocks, tables, heads, nk, tt):
    t, d = h2.shape
    n_chunks, _, te = vt_blocks.shape
    keys_per_step = te // nk
    assert SUBLANE % keys_per_step == 0
    steps_per_tile = SUBLANE // keys_per_step
    c1, p1, r2, p2 = tables
    row_spec = pl.BlockSpec((heads, SUBLANE, tt), lambda i, j: (0, j // steps_per_tile, i))
    tab_spec = pl.BlockSpec((heads * nk, tt), lambda i, j: (0, i))
    return pl.pallas_call(
        functools.partial(_peer_ffn_kernel, nk=nk),
        grid=(t // tt, n_chunks),
        in_specs=[pl.BlockSpec((tt, d), lambda i, j: (i, 0)),
                  pl.BlockSpec((te, d), lambda i, j: (j, 0)),
                  pl.BlockSpec((None, d, te), lambda i, j: (j, 0, 0)),
                  row_spec, row_spec, tab_spec, tab_spec],
        out_specs=pl.BlockSpec((d, tt), lambda i, j: (0, i)),
        out_shape=jax.ShapeDtypeStruct((d, t), F32),
        compiler_params=_params("parallel", "arbitrary"),
        name="peer_ffn",
    )(h2, u_bf16, vt_blocks, c1.reshape(heads, nk, t), p1.reshape(heads, nk, t), r2, p2)


def _residual_kernel(x_ref, pt_ref, gt_ref, g_ref, o_ref, *, final_norm):
    x = x_ref[...] + gt_ref[...] * jnp.transpose(pt_ref[...])
    if final_norm:
        x = (x * lax.rsqrt(jnp.mean(x * x, axis=-1, keepdims=True) + RMS_EPS)) * g_ref[...]
    o_ref[...] = x


def _residual(xt, peer_t, gate, g_final, seq, tm, final_norm):
    t, d = xt.shape
    per_batch = seq // tm
    return pl.pallas_call(
        functools.partial(_residual_kernel, final_norm=final_norm),
        grid=(t // tm,),
        in_specs=[pl.BlockSpec((tm, d), lambda i: (i, 0)),
                  pl.BlockSpec((d, tm), lambda i: (0, i)),
                  pl.BlockSpec((None, 1, d), lambda i: (i // per_batch, 0, 0)),
                  pl.BlockSpec((1, d), lambda i: (0, 0))],
        out_specs=pl.BlockSpec((tm, d), lambda i: (i, 0)),
        out_shape=jax.ShapeDtypeStruct((t, d), F32),
        compiler_params=_params("parallel"),
        name="residual",
    )(xt, peer_t, gate, g_final.reshape(1, d))


def kernel(x, c, w_ada, b_ada, g_mix, w_in, rel_bias, pool_w, pool_scale, w_out, g_ffn, peer_wq, peer_subkeys,
           peer_u, peer_v, g_final):
    batch, seq, d = x.shape
    t = batch * seq
    depth = w_ada.shape[0]
    pool_total = pool_scale.shape[-1]
    att_w = (w_in.shape[-1] - pool_total) // 3
    heads = att_w // HEAD_DIM
    ph, _, nk, _ = peer_subkeys.shape[1:]
    n_exp = peer_u.shape[1]
    tm = min(512, seq)
    tn = min(512, d)
    te = 4 * nk

    xt = x.reshape(t, d)
    c_pad = jnp.pad(c, ((0, SUBLANE - batch % SUBLANE if batch % SUBLANE else 0), (0, 0)))
    bias_tables = _attention_bias_tables(rel_bias)

    for layer in range(depth):
        mod = _ada(c_pad, w_ada, b_ada, layer, tn=min(512, d))[:batch]
        sh1, sc1, gt1, sh2, sc2, gt2 = [m.reshape(batch, 1, d) for m in jnp.split(mod, 6, axis=-1)]

        proj_cm = _inproj(xt, g_mix, sc1, sh1, w_in[layer].astype(BF16), layer, seq, tm, tn)
        attn_hm = _attention(proj_cm, bias_tables, batch, seq, heads)
        xt = _outproj(attn_hm, proj_cm, pool_w[layer].astype(BF16), pool_scale, w_out[layer].astype(BF16),
                      xt, gt1, layer, seq, heads, tm, tn)

        h2, s_t = _peer_q(xt, g_ffn, sc2, sh2, peer_wq[layer].astype(BF16),
                          peer_subkeys[layer].astype(BF16), layer, seq, tm)
        tables = _peer_topk(s_t, ph, nk, tk=min(512, t))
        vt_blocks = jnp.transpose(peer_v[layer].reshape(n_exp // te, te, d), (0, 2, 1)).astype(BF16)
        peer_t = _peer_ffn(h2, peer_u[layer].astype(BF16), vt_blocks, tables, ph, nk, tt=min(512, t))
        xt = _residual(xt, peer_t, gt2, g_final, seq, min(256, seq), final_norm=(layer == depth - 1))

    return xt.reshape(batch, seq, d)
```

```python
import functools
import math

import numpy as np
import jax
import jax.numpy as jnp
from jax import lax
from jax.experimental import pallas as pl
from jax.experimental.pallas import tpu as pltpu

HEAD_DIM = 128
DIL_PAIRS = ((128, 1), (512, 4), (2048, 16))
NUM_BUCKETS = 32
MAX_DISTANCE = 1024
POOL_WINDOWS = (2, 4, 8, 16)
PEER_TOPK = 16
RMS_EPS = 1e-6
NEG_INF = -1e30

LANE = 128
SUBLANE = 8
VMEM_LIMIT_BYTES = 58 * 1024 * 1024

ATT_SIDE = 64
ATT_TQ = 2 * ATT_SIDE
ATT_TK = ATT_TQ + 2 * ATT_SIDE
ATT_MACRO = ATT_TQ * max(d for _, d in DIL_PAIRS)
ATT_VARIANTS = 3
ATT_UNROLL = 16

F32 = jnp.float32
BF16 = jnp.bfloat16
NT_DIMS = (((1,), (1,)), ((), ()))


def _params(*semantics):
    return pltpu.CompilerParams(dimension_semantics=semantics, vmem_limit_bytes=VMEM_LIMIT_BYTES)


def _rows(start, size, stride):
    return pl.ds(start, size) if stride == 1 else pl.ds(start, size, stride=stride)


def _ada_kernel(c_ref, w_ref, b_ref, o_ref):
    s = jax.nn.silu(c_ref[...])
    o_ref[...] = jnp.dot(s, w_ref[...], precision=lax.Precision.HIGHEST,
                         preferred_element_type=F32) + b_ref[...]


def _ada(c_pad, w_ada, b_ada, layer, tn=512):
    rows, d = c_pad.shape
    n = w_ada.shape[-1]
    return pl.pallas_call(
        _ada_kernel,
        grid=(n // tn,),
        in_specs=[pl.BlockSpec((rows, d), lambda j: (0, 0)),
                  pl.BlockSpec((None, d, tn), lambda j: (layer, 0, j)),
                  pl.BlockSpec((None, 1, tn), lambda j: (layer, 0, j))],
        out_specs=pl.BlockSpec((rows, tn), lambda j: (0, j)),
        out_shape=jax.ShapeDtypeStruct((rows, n), F32),
        compiler_params=_params("arbitrary"),
        name="ada",
    )(c_pad, w_ada, b_ada.reshape(b_ada.shape[0], 1, n))


def _modulated_rmsnorm(x, gain, scale, shift):
    y = x * lax.rsqrt(jnp.mean(x * x, axis=-1, keepdims=True) + RMS_EPS)
    return (y * gain) * (1.0 + scale) + shift


def _inproj_kernel(x_ref, g_ref, sc_ref, sh_ref, w_ref, o_ref, h_scr):
    @pl.when(pl.program_id(1) == 0)
    def _():
        h_scr[...] = _modulated_rmsnorm(x_ref[...], g_ref[...], sc_ref[...], sh_ref[...]).astype(BF16)

    acc = jnp.dot(h_scr[...], w_ref[...], preferred_element_type=F32)
    for k in range(o_ref.shape[0]):
        o_ref[k] = acc[:, k * LANE:(k + 1) * LANE]


def _inproj(xt, gain, scale, shift, w_bf16, layer, seq, tm, tn):
    t, d = xt.shape
    n = w_bf16.shape[-1]
    per_batch = seq // tm
    return pl.pallas_call(
        _inproj_kernel,
        grid=(t // tm, n // tn),
        in_specs=[pl.BlockSpec((tm, d), lambda i, j: (i, 0)),
                  pl.BlockSpec((None, 1, d), lambda i, j: (layer, 0, 0)),
                  pl.BlockSpec((None, 1, d), lambda i, j: (i // per_batch, 0, 0)),
                  pl.BlockSpec((None, 1, d), lambda i, j: (i // per_batch, 0, 0)),
                  pl.BlockSpec((d, tn), lambda i, j: (0, j))],
        out_specs=pl.BlockSpec((tn // LANE, tm, LANE), lambda i, j: (j, i, 0)),
        out_shape=jax.ShapeDtypeStruct((n // LANE, t, LANE), F32),
        scratch_shapes=[pltpu.VMEM((tm, d), BF16)],
        compiler_params=_params("parallel", "arbitrary"),
        name="inproj",
    )(xt, gain.reshape(gain.shape[0], 1, d), scale, shift, w_bf16)


def _t5_bucket_np(rel):
    half = NUM_BUCKETS // 2
    n = -rel
    ret = np.where(n < 0, half, 0)
    n = np.abs(n)
    max_exact = half // 2
    nf = np.maximum(n, 1).astype(np.float32)
    large = max_exact + (np.log(nf / np.float32(max_exact)) / np.float32(math.log(MAX_DISTANCE / max_exact))
                         * np.float32(half - max_exact)).astype(np.int32)
    large = np.minimum(large, half - 1)
    return ret + np.where(n < max_exact, n, large)


def _attention_bucket_tables():
    i = np.arange(ATT_TQ)[:, None]
    j = np.arange(ATT_TK)[None, :]
    tables = []
    for _, dil in DIL_PAIRS:
        for off in (0, -ATT_SIDE, -2 * ATT_SIDE):
            rel = j + off - i
            tables.append(np.where(np.abs(rel) <= ATT_SIDE, _t5_bucket_np(rel * dil), -1))
    return np.stack(tables).astype(np.int32)


def _bias_kernel(rb_ref, bucket_ref, o_ref):
    h = pl.program_id(0)
    bucket = bucket_ref[...]
    out = jnp.full(bucket.shape, NEG_INF, F32)
    for b in range(NUM_BUCKETS):
        out = jnp.where(bucket == b, rb_ref[b, h], out)
    o_ref[...] = out


def _attention_bias_tables(rel_bias):
    buckets = jnp.asarray(_attention_bucket_tables())
    heads = rel_bias.shape[1]
    return pl.pallas_call(
        _bias_kernel,
        grid=(heads,),
        in_specs=[pl.BlockSpec(memory_space=pltpu.SMEM),
                  pl.BlockSpec(buckets.shape, lambda h: (0, 0, 0))],
        out_specs=pl.BlockSpec((None,) + buckets.shape, lambda h: (h, 0, 0, 0)),
        out_shape=jax.ShapeDtypeStruct((heads,) + buckets.shape, F32),
        compiler_params=_params("arbitrary"),
        name="attn_bias",
    )(rel_bias, buckets)


def _attn_kernel(q_ref, k_ref, v_ref, bias_ref, o_ref, ob_scr, lse_scr, *, seq):
    scale = HEAD_DIM ** -0.5

    def macro_tile(t, carry):
        p0 = pl.multiple_of(t * ATT_MACRO, ATT_MACRO)
        for g, (_, dil) in enumerate(DIL_PAIRS):
            length = seq // dil
            nblk = ATT_MACRO // (dil * ATT_TQ)

            def block(it, g=g, dil=dil, length=length):
                r = it % dil
                n = it // dil
                q0 = p0 // dil + n * ATT_TQ
                k0 = jnp.clip(q0 - ATT_SIDE, 0, length - ATT_TK)
                variant = jnp.where(q0 == 0, 0, jnp.where(q0 == length - ATT_TQ, 2, 1))
                q = q_ref[_rows(r + q0 * dil, ATT_TQ, dil), :].astype(BF16)
                k = k_ref[_rows(r + k0 * dil, ATT_TK, dil), :].astype(BF16)
                v = v_ref[_rows(r + k0 * dil, ATT_TK, dil), :].astype(BF16)
                s = lax.dot_general(q, k, NT_DIMS, preferred_element_type=F32)
                s = s * scale + bias_ref[g * ATT_VARIANTS + variant]
                m = jnp.max(s, axis=1, keepdims=True)
                p = jnp.exp(s - m)
                l = jnp.sum(p, axis=1, keepdims=True)
                o = jnp.dot(p.astype(BF16), v, preferred_element_type=F32) / l
                lse = m + jnp.log(l)
                dst = _rows(r + n * ATT_TQ * dil, ATT_TQ, dil)
                ob_scr[g, dst, :] = o
                lse_scr[g, dst, :] = jnp.broadcast_to(lse, (ATT_TQ, LANE))

            def block_group(gi, c, block=block):
                for u in range(ATT_UNROLL):
                    block(gi * ATT_UNROLL + u)
                return c

            lax.fori_loop(0, dil * nblk // ATT_UNROLL, block_group, 0)

        chunk = 256

        def merge(ci, c):
            rows = pl.ds(pl.multiple_of(ci * chunk, chunk), chunk)
            l0, l1, l2 = lse_scr[0, rows, :], lse_scr[1, rows, :], lse_scr[2, rows, :]
            mx = jnp.maximum(jnp.maximum(l0, l1), l2)
            e0, e1, e2 = jnp.exp(l0 - mx), jnp.exp(l1 - mx), jnp.exp(l2 - mx)
            den = e0 + e1 + e2
            out = (e0 / den) * ob_scr[0, rows, :] + (e1 / den) * ob_scr[1, rows, :] + (e2 / den) * ob_scr[2, rows, :]
            o_ref[pl.ds(p0 + pl.multiple_of(ci * chunk, chunk), chunk), :] = out.astype(o_ref.dtype)
            return c

        lax.fori_loop(0, ATT_MACRO // chunk, merge, 0)
        return carry

    lax.fori_loop(0, seq // ATT_MACRO, macro_tile, 0)


def _attention(proj_cm, bias_tables, batch, seq, heads):
    t = proj_cm.shape[1]
    assert seq % ATT_MACRO == 0 and seq // max(d for _, d in DIL_PAIRS) >= ATT_TK
    nb = len(DIL_PAIRS)

    def qkv_spec(off):
        return pl.BlockSpec((None, seq, LANE), lambda b, h, off=off: (off + h, b, 0))

    return pl.pallas_call(
        functools.partial(_attn_kernel, seq=seq),
        grid=(batch, heads),
        in_specs=[qkv_spec(0), qkv_spec(heads), qkv_spec(2 * heads),
                  pl.BlockSpec((None, nb * ATT_VARIANTS, ATT_TQ, ATT_TK), lambda b, h: (h, 0, 0, 0))],
        out_specs=pl.BlockSpec((None, seq, LANE), lambda b, h: (h, b, 0)),
        out_shape=jax.ShapeDtypeStruct((heads, t, LANE), BF16),
        scratch_shapes=[pltpu.VMEM((nb, ATT_MACRO, LANE), F32),
                        pltpu.VMEM((nb, ATT_MACRO, LANE), F32)],
        compiler_params=_params("parallel", "arbitrary"),
        name="attn",
    )(proj_cm, proj_cm, proj_cm, bias_tables)


def _outproj_kernel(attn_ref, p_ref, prev_ref, next_ref, pw_ref, ps_ref, w_ref, x_ref, gt_ref, o_ref,
                    lhs_scr, pooled_scr, ext_scr, *, seq, tm):
    i = pl.program_id(0)
    heads = attn_ref.shape[0]
    att_w = heads * LANE
    n_pool_blocks = p_ref.shape[0]
    pool_group = pw_ref.shape[-1]

    @pl.when(pl.program_id(1) == 0)
    def _():
        for h in range(heads):
            lhs_scr[:, h * LANE:(h + 1) * LANE] = attn_ref[h]

        base = (i * tm) % seq
        has_prev = (base != 0).astype(F32)
        has_next = (base + tm != seq).astype(F32)
        pos = base + lax.broadcasted_iota(jnp.int32, (tm, LANE), 0)
        for cb in range(n_pool_blocks):
            w = POOL_WINDOWS[(cb * LANE) // pool_group]
            main = p_ref[cb]
            ext_scr[0:SUBLANE, :] = prev_ref[cb] * has_prev
            ext_scr[SUBLANE:SUBLANE + tm, :] = main
            ext_scr[SUBLANE + tm:2 * SUBLANE + tm, :] = next_ref[cb] * has_next
            total = ext_scr[pl.ds(SUBLANE - w // 2, tm), :]
            for off in range(-w // 2 + 1, w // 2):
                total = total + ext_scr[pl.ds(SUBLANE + off, tm), :]
            count = (jnp.minimum(pos + w // 2, seq) - jnp.maximum(pos - w // 2, 0)).astype(F32)
            pooled_scr[:, cb * LANE:(cb + 1) * LANE] = (total / count - main).astype(BF16)
        for gi in range(len(POOL_WINDOWS)):
            cols = slice(gi * pool_group, (gi + 1) * pool_group)
            mixed = jnp.dot(pooled_scr[:, cols], pw_ref[gi], preferred_element_type=F32)
            lhs_scr[:, att_w + gi * pool_group:att_w + (gi + 1) * pool_group] = (mixed * ps_ref[:, cols]).astype(BF16)

    mix = jnp.dot(lhs_scr[...], w_ref[...], preferred_element_type=F32)
    o_ref[...] = x_ref[...] + gt_ref[...] * mix


def _outproj(attn_hm, proj_cm, pool_w_bf16, pool_scale, w_out_bf16, xt, gate, layer, seq, heads, tm, tn):
    t, d = xt.shape
    pool_w_total = pool_scale.shape[-1]
    npb = pool_w_total // LANE
    pool_blk = (3 * heads) // npb
    assert pool_blk * npb == 3 * heads and tm % SUBLANE == 0
    per_batch = seq // tm
    halo_per_tile = tm // SUBLANE
    n_halo = t // SUBLANE
    mix_w = w_out_bf16.shape[0]
    return pl.pallas_call(
        functools.partial(_outproj_kernel, seq=seq, tm=tm),
        grid=(t // tm, d // tn),
        in_specs=[pl.BlockSpec((heads, tm, LANE), lambda i, j: (0, i, 0)),
                  pl.BlockSpec((npb, tm, LANE), lambda i, j: (pool_blk, i, 0)),
                  pl.BlockSpec((npb, SUBLANE, LANE),
                               lambda i, j: (pool_blk, jnp.maximum(i * halo_per_tile - 1, 0), 0)),
                  pl.BlockSpec((npb, SUBLANE, LANE),
                               lambda i, j: (pool_blk, jnp.minimum((i + 1) * halo_per_tile, n_halo - 1), 0)),
                  pl.BlockSpec(pool_w_bf16.shape, lambda i, j: (0, 0, 0)),
                  pl.BlockSpec((None, 1, pool_w_total), lambda i, j: (layer, 0, 0)),
                  pl.BlockSpec((mix_w, tn), lambda i, j: (0, j)),
                  pl.BlockSpec((tm, tn), lambda i, j: (i, j)),
                  pl.BlockSpec((None, 1, tn), lambda i, j: (i // per_batch, 0, j))],
        out_specs=pl.BlockSpec((tm, tn), lambda i, j: (i, j)),
        out_shape=jax.ShapeDtypeStruct((t, d), F32),
        scratch_shapes=[pltpu.VMEM((tm, mix_w), BF16),
                        pltpu.VMEM((tm, pool_w_total), BF16),
                        pltpu.VMEM((tm + 2 * SUBLANE, LANE), F32)],
        compiler_params=_params("parallel", "arbitrary"),
        name="outproj",
    )(attn_hm, proj_cm, proj_cm, proj_cm, pool_w_bf16,
      pool_scale.reshape(pool_scale.shape[0], 1, pool_w_total), w_out_bf16, xt, gate)


def _peer_q_kernel(x_ref, g_ref, sc_ref, sh_ref, wq_ref, sk_ref, h_ref, s_ref):
    @pl.when(pl.program_id(1) == 0)
    def _():
        h_ref[...] = _modulated_rmsnorm(x_ref[...], g_ref[...], sc_ref[...], sh_ref[...]).astype(BF16)

    q = jnp.dot(h_ref[...], wq_ref[...], preferred_element_type=F32).astype(BF16)
    hps, _, nk, half = sk_ref.shape
    for hh in range(hps):
        for p in range(2):
            blk = hh * 2 + p
            s_ref[blk * nk:(blk + 1) * nk, :] = lax.dot_general(
                sk_ref[hh, p], q[:, blk * half:(blk + 1) * half], NT_DIMS, preferred_element_type=F32)


def _peer_q(xt, gain, scale, shift, wq_bf16, subkeys_bf16, layer, seq, tm):
    t, d = xt.shape
    ph, _, nk, half = subkeys_bf16.shape
    per_batch = seq // tm
    hps = 2 if ph % 2 == 0 else 1
    return pl.pallas_call(
        _peer_q_kernel,
        grid=(t // tm, ph // hps),
        in_specs=[pl.BlockSpec((tm, d), lambda i, h: (i, 0)),
                  pl.BlockSpec((None, 1, d), lambda i, h: (layer, 0, 0)),
                  pl.BlockSpec((None, 1, d), lambda i, h: (i // per_batch, 0, 0)),
                  pl.BlockSpec((None, 1, d), lambda i, h: (i // per_batch, 0, 0)),
                  pl.BlockSpec((d, hps * 2 * half), lambda i, h: (0, h)),
                  pl.BlockSpec((hps, 2, nk, half), lambda i, h: (h, 0, 0, 0))],
        out_specs=[pl.BlockSpec((tm, d), lambda i, h: (i, 0)),
                   pl.BlockSpec((hps * 2 * nk, tm), lambda i, h: (h, i))],
        out_shape=[jax.ShapeDtypeStruct((t, d), BF16),
                   jax.ShapeDtypeStruct((ph * 2 * nk, t), F32)],
        compiler_params=_params("parallel", "arbitrary"),
        name="peer_q",
    )(xt, gain.reshape(gain.shape[0], 1, d), scale, shift, wq_bf16, subkeys_bf16)


def _ranked_top(v, index):
    rank = jnp.full(v.shape, float(PEER_TOPK), F32)
    tops = []
    for k in range(PEER_TOPK):
        m = jnp.max(v, axis=0, keepdims=True)
        first = jnp.min(jnp.where(v == m, index, float(v.shape[0])), axis=0, keepdims=True)
        hit = index == first
        v = jnp.where(hit, -jnp.inf, v)
        rank = jnp.where(hit, float(k), rank)
        tops.append(m)
    return rank, tops


def _pair_counts(top1, top2):
    lanes = top1[0].shape[1]
    t1 = jnp.concatenate(top1, axis=0)
    a_iota = lax.broadcasted_iota(jnp.int32, (PEER_TOPK, lanes), 0).astype(F32)
    vals, idxs = [], []
    for b in range(PEER_TOPK):
        n_a = PEER_TOPK // (b + 1)
        rows = PEER_TOPK if n_a > SUBLANE else SUBLANE
        val = t1[:rows] + top2[b]
        if n_a < rows:
            val = jnp.where(a_iota[:rows] < float(n_a), val, -jnp.inf)
        vals.append(val)
        idxs.append(a_iota[:rows] * float(PEER_TOPK) + float(b))
    cand = jnp.concatenate(vals, axis=0)
    index = jnp.concatenate(idxs, axis=0)
    hits = jnp.zeros(cand.shape, F32)
    best = top1[0] + top2[0]
    z = jnp.zeros((1, lanes), F32)
    for _ in range(PEER_TOPK):
        m = jnp.max(cand, axis=0, keepdims=True)
        first = jnp.min(jnp.where(cand == m, index, float(PEER_TOPK * PEER_TOPK)), axis=0, keepdims=True)
        hit = index == first
        cand = jnp.where(hit, -jnp.inf, cand)
        hits = hits + jnp.where(hit, 1.0, 0.0)
        z = z + jnp.exp(m - best)
    cnt_lo = hits[0:SUBLANE]
    for b in range(1, PEER_TOPK):
        start = PEER_TOPK + (b - 1) * SUBLANE
        cnt_lo = cnt_lo + hits[start:start + SUBLANE]
    return jnp.concatenate([cnt_lo, hits[SUBLANE:PEER_TOPK]], axis=0), z


def _topk_kernel(s_ref, c1_ref, p1_ref, r2_ref, p2_ref):
    nk = s_ref.shape[0] // 2
    index = lax.broadcasted_iota(jnp.int32, (nk, LANE), 0).astype(F32)
    for strip in range(s_ref.shape[1] // LANE):
        cols = slice(strip * LANE, (strip + 1) * LANE)
        s1 = s_ref[0:nk, cols]
        s2 = s_ref[nk:2 * nk, cols]
        rank1, top1 = _ranked_top(s1, index)
        rank2, top2 = _ranked_top(s2, index)
        cnt, z = _pair_counts(top1, top2)
        c1 = jnp.zeros(s1.shape, F32)
        for a in range(PEER_TOPK):
            c1 = jnp.where(rank1 == float(a), cnt[a:a + 1], c1)
        c1_ref[:, cols] = c1
        p1_ref[:, cols] = jnp.exp(s1 - top1[0]) * (1.0 / z)
        r2_ref[:, cols] = rank2.astype(r2_ref.dtype)
        p2_ref[:, cols] = jnp.exp(s2 - top2[0]).astype(p2_ref.dtype)


def _peer_topk(s_t, ph, nk, tk):
    t = s_t.shape[1]
    out_spec = pl.BlockSpec((nk, tk), lambda h, i: (h, i))
    return pl.pallas_call(
        _topk_kernel,
        grid=(ph, t // tk),
        in_specs=[pl.BlockSpec((2 * nk, tk), lambda h, i: (h, i))],
        out_specs=[out_spec] * 4,
        out_shape=[jax.ShapeDtypeStruct((ph * nk, t), dt) for dt in (F32, F32, BF16, BF16)],
        compiler_params=_params("parallel", "parallel"),
        name="peer_topk",
    )(s_t)


def _peer_ffn_kernel(h_ref, u_ref, vt_ref, c1_ref, p1_ref, r2_ref, p2_ref, o_ref, *, nk):
    j = pl.program_id(1)
    te = u_ref.shape[0]
    tt = h_ref.shape[0]
    heads = c1_ref.shape[0]
    keys_per_step = te // nk
    steps_per_tile = SUBLANE // keys_per_step

    @pl.when(j == 0)
    def _():
        o_ref[...] = jnp.zeros_like(o_ref)

    a = lax.dot_general(u_ref[...], h_ref[...], NT_DIMS, preferred_element_type=F32)
    act = jax.nn.gelu(a).astype(BF16)
    zero = jnp.zeros((nk, tt), BF16)
    parts = []
    for e in range(keys_per_step):
        row = (j % steps_per_tile) * keys_per_step + e
        gate = zero
        for h in range(heads):
            c1 = jnp.broadcast_to(c1_ref[h, pl.ds(row, 1), :].astype(BF16), (nk, tt))
            p1 = jnp.broadcast_to(p1_ref[h, pl.ds(row, 1), :].astype(BF16), (nk, tt))
            r2 = r2_ref[h * nk:(h + 1) * nk, :]
            p2 = p2_ref[h * nk:(h + 1) * nk, :]
            gate = gate + jnp.where(r2 < c1, p1 * p2, zero)
        parts.append(act[e * nk:(e + 1) * nk] * gate)
    act_bf16 = parts[0] if len(parts) == 1 else jnp.concatenate(parts, axis=0)
    o_ref[...] += jnp.dot(vt_ref[...], act_bf16, preferred_element_type=F32)


def _peer_ffn(h2, u_bf16, vt_blocks, tables, heads, nk, tt):
    t, d = h2.shape
    n_chunks, _, te = vt_blocks.shape
    keys_per_step = te // nk
    assert SUBLANE % keys_per_step == 0
    steps_per_tile = SUBLANE // keys_per_step
    c1, p1, r2, p2 = tables
    row_spec = pl.BlockSpec((heads, SUBLANE, tt), lambda i, j: (0, j // steps_per_tile, i))
    tab_spec = pl.BlockSpec((heads * nk, tt), lambda i, j: (0, i))
    return pl.pallas_call(
        functools.partial(_peer_ffn_kernel, nk=nk),
        grid=(t // tt, n_chunks),
        in_specs=[pl.BlockSpec((tt, d), lambda i, j: (i, 0)),
                  pl.BlockSpec((te, d), lambda i, j: (j, 0)),
                  pl.BlockSpec((None, d, te), lambda i, j: (j, 0, 0)),
                  row_spec, row_spec, tab_spec, tab_spec],
        out_specs=pl.BlockSpec((d, tt), lambda i, j: (0, i)),
        out_shape=jax.ShapeDtypeStruct((d, t), F32),
        compiler_params=_params("parallel", "arbitrary"),
        name="peer_ffn",
    )(h2, u_bf16, vt_blocks, c1.reshape(heads, nk, t), p1.reshape(heads, nk, t), r2, p2)


def _residual_kernel(x_ref, pt_ref, gt_ref, g_ref, o_ref, *, final_norm):
    x = x_ref[...] + gt_ref[...] * jnp.transpose(pt_ref[...])
    if final_norm:
        x = (x * lax.rsqrt(jnp.mean(x * x, axis=-1, keepdims=True) + RMS_EPS)) * g_ref[...]
    o_ref[...] = x


def _residual(xt, peer_t, gate, g_final, seq, tm, final_norm):
    t, d = xt.shape
    per_batch = seq // tm
    return pl.pallas_call(
        functools.partial(_residual_kernel, final_norm=final_norm),
        grid=(t // tm,),
        in_specs=[pl.BlockSpec((tm, d), lambda i: (i, 0)),
                  pl.BlockSpec((d, tm), lambda i: (0, i)),
                  pl.BlockSpec((None, 1, d), lambda i: (i // per_batch, 0, 0)),
                  pl.BlockSpec((1, d), lambda i: (0, 0))],
        out_specs=pl.BlockSpec((tm, d), lambda i: (i, 0)),
        out_shape=jax.ShapeDtypeStruct((t, d), F32),
        compiler_params=_params("parallel"),
        name="residual",
    )(xt, peer_t, gate, g_final.reshape(1, d))


def kernel(x, c, w_ada, b_ada, g_mix, w_in, rel_bias, pool_w, pool_scale, w_out, g_ffn, peer_wq, peer_subkeys,
           peer_u, peer_v, g_final):
    batch, seq, d = x.shape
    t = batch * seq
    depth = w_ada.shape[0]
    pool_total = pool_scale.shape[-1]
    att_w = (w_in.shape[-1] - pool_total) // 3
    heads = att_w // HEAD_DIM
    ph, _, nk, _ = peer_subkeys.shape[1:]
    n_exp = peer_u.shape[1]
    tm = min(512, seq)
    tn = min(512, d)
    te = 4 * nk

    xt = x.reshape(t, d)
    c_pad = jnp.pad(c, ((0, SUBLANE - batch % SUBLANE if batch % SUBLANE else 0), (0, 0)))
    bias_tables = _attention_bias_tables(rel_bias)

    for layer in range(depth):
        mod = _ada(c_pad, w_ada, b_ada, layer, tn=min(512, d))[:batch]
        sh1, sc1, gt1, sh2, sc2, gt2 = [m.reshape(batch, 1, d) for m in jnp.split(mod, 6, axis=-1)]

        proj_cm = _inproj(xt, g_mix, sc1, sh1, w_in[layer].astype(BF16), layer, seq, tm, tn)
        attn_hm = _attention(proj_cm, bias_tables, batch, seq, heads)
        xt = _outproj(attn_hm, proj_cm, pool_w[layer].astype(BF16), pool_scale, w_out[layer].astype(BF16),
                      xt, gt1, layer, seq, heads, tm, tn)

        h2, s_t = _peer_q(xt, g_ffn, sc2, sh2, peer_wq[layer].astype(BF16),
                          peer_subkeys[layer].astype(BF16), layer, seq, tm)
        tables = _peer_topk(s_t, ph, nk, tk=min(512, t))
        vt_blocks = jnp.transpose(peer_v[layer].reshape(n_exp // te, te, d), (0, 2, 1)).astype(BF16)
        peer_t = _peer_ffn(h2, peer_u[layer].astype(BF16), vt_blocks, tables, ph, nk, tt=min(512, t))
        xt = _residual(xt, peer_t, gt2, g_final, seq, min(256, seq), final_norm=(layer == depth - 1))

    return xt.reshape(batch, seq, d)
```

```python
import functools
import math

import numpy as np
import jax
import jax.numpy as jnp
from jax import lax
from jax.experimental import pallas as pl
from jax.experimental.pallas import tpu as pltpu

HEAD_DIM = 128
DIL_PAIRS = ((128, 1), (512, 4), (2048, 16))
NUM_BUCKETS = 32
MAX_DISTANCE = 1024
POOL_WINDOWS = (2, 4, 8, 16)
PEER_TOPK = 16
RMS_EPS = 1e-6
NEG_INF = -1e30

LANE = 128
SUBLANE = 8
BF16_SUBLANES = 16
VMEM_LIMIT_BYTES = 58 * 1024 * 1024

ATT_SIDE = 64
ATT_TQ = 2 * ATT_SIDE
ATT_TK = ATT_TQ + 2 * ATT_SIDE
ATT_MACRO = ATT_TQ * max(d for _, d in DIL_PAIRS)
ATT_VARIANTS = 3
ATT_UNROLL = 16

F32 = jnp.float32
BF16 = jnp.bfloat16
NT_DIMS = (((1,), (1,)), ((), ()))


def _params(*semantics):
    return pltpu.CompilerParams(dimension_semantics=semantics, vmem_limit_bytes=VMEM_LIMIT_BYTES)


def _rows(start, size, stride):
    return pl.ds(start, size) if stride == 1 else pl.ds(start, size, stride=stride)


def _ada_kernel(ct_ref, w_ref, b_ref, o_ref, acc_scr):
    k = pl.program_id(0)
    batch = ct_ref.shape[1]
    tk, n = w_ref.shape

    @pl.when(k == 0)
    def _():
        acc_scr[...] = jnp.zeros_like(acc_scr)

    s = jax.nn.silu(ct_ref[...])
    s_cols = [jnp.broadcast_to(s[:, b:b + 1], (tk, LANE)) for b in range(batch)]
    for lt in range(n // LANE):
        cols = slice(lt * LANE, (lt + 1) * LANE)
        w = w_ref[:, cols]
        for b in range(batch):
            prod = w * s_cols[b]
            acc_scr[b, :, cols] += jnp.sum(prod.reshape(tk // SUBLANE, SUBLANE, LANE), axis=0)

    @pl.when(k == pl.num_programs(0) - 1)
    def _():
        o_ref[...] = jnp.zeros_like(o_ref)
        for b in range(batch):
            o_ref[b:b + 1, :] = jnp.sum(acc_scr[b], axis=0, keepdims=True) + b_ref[...]


def _ada(c, w_ada, b_ada, layer, tk=128):
    batch, d = c.shape
    n = w_ada.shape[-1]
    rows = -(-batch // SUBLANE) * SUBLANE
    return pl.pallas_call(
        _ada_kernel,
        grid=(d // tk,),
        in_specs=[pl.BlockSpec((tk, batch), lambda k: (k, 0)),
                  pl.BlockSpec((None, tk, n), lambda k: (layer, k, 0)),
                  pl.BlockSpec((None, 1, n), lambda k: (layer, 0, 0))],
        out_specs=pl.BlockSpec((rows, n), lambda k: (0, 0)),
        out_shape=jax.ShapeDtypeStruct((rows, n), F32),
        scratch_shapes=[pltpu.VMEM((batch, SUBLANE, n), F32)],
        compiler_params=_params("arbitrary"),
        name="ada",
    )(jnp.transpose(c), w_ada, b_ada.reshape(b_ada.shape[0], 1, n))[:batch]


NORM_ROWS = 16


def _modulated_rmsnorm_rows(x_ref, g_ref, sc_ref, sh_ref, o_ref):
    def body(c, carry):
        rows = pl.ds(pl.multiple_of(c * NORM_ROWS, NORM_ROWS), NORM_ROWS)
        x = x_ref[rows, :]
        y = x * lax.rsqrt(jnp.mean(x * x, axis=-1, keepdims=True) + RMS_EPS)
        o_ref[rows, :] = ((y * g_ref[...]) * (1.0 + sc_ref[...]) + sh_ref[...]).astype(o_ref.dtype)
        return carry

    lax.fori_loop(0, x_ref.shape[0] // NORM_ROWS, body, 0, unroll=4)


def _inproj_kernel(x_ref, g_ref, sc_ref, sh_ref, w_ref, o_ref, h_scr):
    @pl.when(pl.program_id(1) == 0)
    def _():
        _modulated_rmsnorm_rows(x_ref, g_ref, sc_ref, sh_ref, h_scr)

    blocks = o_ref.shape[0] // 2
    for half in range(2):
        acc = jnp.dot(h_scr[...], w_ref[:, half * blocks * LANE:(half + 1) * blocks * LANE],
                      preferred_element_type=F32)
        for k in range(blocks):
            o_ref[half * blocks + k] = acc[:, k * LANE:(k + 1) * LANE]


def _inproj(xt, gain, scale, shift, w_bf16, layer, seq, tm, tn):
    t, d = xt.shape
    n = w_bf16.shape[-1]
    per_batch = seq // tm
    return pl.pallas_call(
        _inproj_kernel,
        grid=(t // tm, n // tn),
        in_specs=[pl.BlockSpec((tm, d), lambda i, j: (i, 0)),
                  pl.BlockSpec((None, 1, d), lambda i, j: (layer, 0, 0)),
                  pl.BlockSpec((None, 1, d), lambda i, j: (i // per_batch, 0, 0)),
                  pl.BlockSpec((None, 1, d), lambda i, j: (i // per_batch, 0, 0)),
                  pl.BlockSpec((d, tn), lambda i, j: (0, j))],
        out_specs=pl.BlockSpec((tn // LANE, tm, LANE), lambda i, j: (j, i, 0)),
        out_shape=jax.ShapeDtypeStruct((n // LANE, t, LANE), F32),
        scratch_shapes=[pltpu.VMEM((tm, d), BF16)],
        compiler_params=_params("parallel", "arbitrary"),
        name="inproj",
    )(xt, gain.reshape(gain.shape[0], 1, d), scale, shift, w_bf16)


def _t5_bucket_np(rel):
    half = NUM_BUCKETS // 2
    n = -rel
    ret = np.where(n < 0, half, 0)
    n = np.abs(n)
    max_exact = half // 2
    nf = np.maximum(n, 1).astype(np.float32)
    large = max_exact + (np.log(nf / np.float32(max_exact)) / np.float32(math.log(MAX_DISTANCE / max_exact))
                         * np.float32(half - max_exact)).astype(np.int32)
    large = np.minimum(large, half - 1)
    return ret + np.where(n < max_exact, n, large)


def _attention_bucket_tables():
    i = np.arange(ATT_TQ)[:, None]
    j = np.arange(ATT_TK)[None, :]
    tables = []
    for _, dil in DIL_PAIRS:
        for off in (0, -ATT_SIDE, -2 * ATT_SIDE):
            rel = j + off - i
            tables.append(np.where(np.abs(rel) <= ATT_SIDE, _t5_bucket_np(rel * dil), -1))
    return np.stack(tables).astype(np.int32)


def _bias_kernel(rb_ref, bucket_ref, o_ref):
    h = pl.program_id(0)
    bucket = bucket_ref[...]
    out = jnp.full(bucket.shape, NEG_INF, F32)
    for b in range(NUM_BUCKETS):
        out = jnp.where(bucket == b, rb_ref[b, h], out)
    o_ref[...] = out


def _attention_bias_tables(rel_bias):
    buckets = jnp.asarray(_attention_bucket_tables())
    heads = rel_bias.shape[1]
    return pl.pallas_call(
        _bias_kernel,
        grid=(heads,),
        in_specs=[pl.BlockSpec(memory_space=pltpu.SMEM),
                  pl.BlockSpec(buckets.shape, lambda h: (0, 0, 0))],
        out_specs=pl.BlockSpec((None,) + buckets.shape, lambda h: (h, 0, 0, 0)),
        out_shape=jax.ShapeDtypeStruct((heads,) + buckets.shape, F32),
        compiler_params=_params("arbitrary"),
        name="attn_bias",
    )(rel_bias, buckets)


def _attn_kernel(q_ref, k_ref, v_ref, bias_ref, o_ref, ob_scr, lse_scr, *, seq):
    scale = HEAD_DIM ** -0.5

    def macro_tile(t, carry):
        p0 = pl.multiple_of(t * ATT_MACRO, ATT_MACRO)
        for g, (_, dil) in enumerate(DIL_PAIRS):
            length = seq // dil
            nblk = ATT_MACRO // (dil * ATT_TQ)

            def block(it, g=g, dil=dil, length=length):
                r = it % dil
                n = it // dil
                q0 = p0 // dil + n * ATT_TQ
                k0 = jnp.clip(q0 - ATT_SIDE, 0, length - ATT_TK)
                variant = jnp.where(q0 == 0, 0, jnp.where(q0 == length - ATT_TQ, 2, 1))
                q = q_ref[_rows(r + q0 * dil, ATT_TQ, dil), :].astype(BF16)
                k = k_ref[_rows(r + k0 * dil, ATT_TK, dil), :].astype(BF16)
                v = v_ref[_rows(r + k0 * dil, ATT_TK, dil), :].astype(BF16)
                s = lax.dot_general(q, k, NT_DIMS, preferred_element_type=F32)
                s = s * scale + bias_ref[g * ATT_VARIANTS + variant]
                m = jnp.max(s, axis=1, keepdims=True)
                p = jnp.exp(s - m)
                l = jnp.sum(p, axis=1, keepdims=True)
                o = jnp.dot(p.astype(BF16), v, preferred_element_type=F32) / l
                lse = m + jnp.log(l)
                dst = _rows(r + n * ATT_TQ * dil, ATT_TQ, dil)
                ob_scr[g, dst, :] = o
                lse_scr[g, dst, :] = jnp.broadcast_to(lse, (ATT_TQ, LANE))

            def block_group(gi, c, block=block):
                for u in range(ATT_UNROLL):
                    block(gi * ATT_UNROLL + u)
                return c

            lax.fori_loop(0, dil * nblk // ATT_UNROLL, block_group, 0)

        chunk = 256

        def merge(ci, c):
            rows = pl.ds(pl.multiple_of(ci * chunk, chunk), chunk)
            l0, l1, l2 = lse_scr[0, rows, :], lse_scr[1, rows, :], lse_scr[2, rows, :]
            mx = jnp.maximum(jnp.maximum(l0, l1), l2)
            e0, e1, e2 = jnp.exp(l0 - mx), jnp.exp(l1 - mx), jnp.exp(l2 - mx)
            den = e0 + e1 + e2
            out = (e0 / den) * ob_scr[0, rows, :] + (e1 / den) * ob_scr[1, rows, :] + (e2 / den) * ob_scr[2, rows, :]
            o_ref[pl.ds(p0 + pl.multiple_of(ci * chunk, chunk), chunk), :] = out.astype(o_ref.dtype)
            return c

        lax.fori_loop(0, ATT_MACRO // chunk, merge, 0)
        return carry

    lax.fori_loop(0, seq // ATT_MACRO, macro_tile, 0)


def _attention(proj_cm, bias_tables, batch, seq, heads):
    t = proj_cm.shape[1]
    assert seq % ATT_MACRO == 0 and seq // max(d for _, d in DIL_PAIRS) >= ATT_TK
    nb = len(DIL_PAIRS)

    def qkv_spec(off):
        return pl.BlockSpec((None, seq, LANE), lambda b, h, off=off: (off + h, b, 0))

    return pl.pallas_call(
        functools.partial(_attn_kernel, seq=seq),
        grid=(batch, heads),
        in_specs=[qkv_spec(0), qkv_spec(heads), qkv_spec(2 * heads),
                  pl.BlockSpec((None, nb * ATT_VARIANTS, ATT_TQ, ATT_TK), lambda b, h: (h, 0, 0, 0))],
        out_specs=pl.BlockSpec((None, seq, LANE), lambda b, h: (h, b, 0)),
        out_shape=jax.ShapeDtypeStruct((heads, t, LANE), BF16),
        scratch_shapes=[pltpu.VMEM((nb, ATT_MACRO, LANE), F32),
                        pltpu.VMEM((nb, ATT_MACRO, LANE), F32)],
        compiler_params=_params("parallel", "arbitrary"),
        name="attn",
    )(proj_cm, proj_cm, proj_cm, bias_tables)


def _outproj_kernel(attn_ref, p_ref, prev_ref, next_ref, pw_ref, ps_ref, w_ref, x_ref, gt_ref, o_ref,
                    lhs_scr, pooled_scr, ext_scr, *, seq, tm):
    i = pl.program_id(0)
    heads = attn_ref.shape[0]
    att_w = heads * LANE
    n_pool_blocks = p_ref.shape[0]
    pool_group = pw_ref.shape[-1]

    @pl.when(pl.program_id(1) == 0)
    def _():
        for h in range(heads):
            lhs_scr[:, h * LANE:(h + 1) * LANE] = attn_ref[h]

        base = (i * tm) % seq
        has_prev = (base != 0).astype(F32)
        has_next = (base + tm != seq).astype(F32)
        pos = base + lax.broadcasted_iota(jnp.int32, (tm, LANE), 0)
        for cb in range(n_pool_blocks):
            w = POOL_WINDOWS[(cb * LANE) // pool_group]
            main = p_ref[cb]
            ext_scr[0:SUBLANE, :] = prev_ref[cb] * has_prev
            ext_scr[SUBLANE:SUBLANE + tm, :] = main
            ext_scr[SUBLANE + tm:2 * SUBLANE + tm, :] = next_ref[cb] * has_next
            total = ext_scr[pl.ds(SUBLANE - w // 2, tm), :]
            for off in range(-w // 2 + 1, w // 2):
                total = total + ext_scr[pl.ds(SUBLANE + off, tm), :]
            count = (jnp.minimum(pos + w // 2, seq) - jnp.maximum(pos - w // 2, 0)).astype(F32)
            pooled_scr[:, cb * LANE:(cb + 1) * LANE] = (total / count - main).astype(BF16)
        for gi in range(len(POOL_WINDOWS)):
            cols = slice(gi * pool_group, (gi + 1) * pool_group)
            mixed = jnp.dot(pooled_scr[:, cols], pw_ref[gi], preferred_element_type=F32)
            lhs_scr[:, att_w + gi * pool_group:att_w + (gi + 1) * pool_group] = (mixed * ps_ref[:, cols]).astype(BF16)

    half_w = o_ref.shape[1] // 2
    for half in range(2):
        cols = slice(half * half_w, (half + 1) * half_w)
        mix = jnp.dot(lhs_scr[...], w_ref[:, cols], preferred_element_type=F32)
        o_ref[:, cols] = x_ref[:, cols] + gt_ref[:, cols] * mix


def _outproj(attn_hm, proj_cm, pool_w_bf16, pool_scale, w_out_bf16, xt, gate, layer, seq, heads, tm, tn):
    t, d = xt.shape
    pool_w_total = pool_scale.shape[-1]
    npb = pool_w_total // LANE
    pool_blk = (3 * heads) // npb
    assert pool_blk * npb == 3 * heads and tm % SUBLANE == 0
    per_batch = seq // tm
    halo_per_tile = tm // SUBLANE
    n_halo = t // SUBLANE
    mix_w = w_out_bf16.shape[0]
    return pl.pallas_call(
        functools.partial(_outproj_kernel, seq=seq, tm=tm),
        grid=(t // tm, d // tn),
        in_specs=[pl.BlockSpec((heads, tm, LANE), lambda i, j: (0, i, 0)),
                  pl.BlockSpec((npb, tm, LANE), lambda i, j: (pool_blk, i, 0)),
                  pl.BlockSpec((npb, SUBLANE, LANE),
                               lambda i, j: (pool_blk, jnp.maximum(i * halo_per_tile - 1, 0), 0)),
                  pl.BlockSpec((npb, SUBLANE, LANE),
                               lambda i, j: (pool_blk, jnp.minimum((i + 1) * halo_per_tile, n_halo - 1), 0)),
                  pl.BlockSpec(pool_w_bf16.shape, lambda i, j: (0, 0, 0)),
                  pl.BlockSpec((None, 1, pool_w_total), lambda i, j: (layer, 0, 0)),
                  pl.BlockSpec((mix_w, tn), lambda i, j: (0, j)),
                  pl.BlockSpec((tm, tn), lambda i, j: (i, j)),
                  pl.BlockSpec((None, 1, tn), lambda i, j: (i // per_batch, 0, j))],
        out_specs=pl.BlockSpec((tm, tn), lambda i, j: (i, j)),
        out_shape=jax.ShapeDtypeStruct((t, d), F32),
        scratch_shapes=[pltpu.VMEM((tm, mix_w), BF16),
                        pltpu.VMEM((tm, pool_w_total), BF16),
                        pltpu.VMEM((tm + 2 * SUBLANE, LANE), F32)],
        compiler_params=_params("parallel", "arbitrary"),
        name="outproj",
    )(attn_hm, proj_cm, proj_cm, proj_cm, pool_w_bf16,
      pool_scale.reshape(pool_scale.shape[0], 1, pool_w_total), w_out_bf16, xt, gate)


def _peer_q_kernel(x_ref, g_ref, sc_ref, sh_ref, wq_ref, sk_ref, h_ref, s_ref):
    @pl.when(pl.program_id(1) == 0)
    def _():
        _modulated_rmsnorm_rows(x_ref, g_ref, sc_ref, sh_ref, h_ref)

    q = jnp.dot(h_ref[...], wq_ref[...], preferred_element_type=F32).astype(BF16)
    hps, _, nk, half = sk_ref.shape
    for hh in range(hps):
        for p in range(2):
            blk = hh * 2 + p
            s_ref[blk * nk:(blk + 1) * nk, :] = lax.dot_general(
                sk_ref[hh, p], q[:, blk * half:(blk + 1) * half], NT_DIMS, preferred_element_type=F32)


def _peer_q(xt, gain, scale, shift, wq_bf16, subkeys_bf16, layer, seq, tm):
    t, d = xt.shape
    ph, _, nk, half = subkeys_bf16.shape
    per_batch = seq // tm
    hps = 2 if ph % 2 == 0 else 1
    return pl.pallas_call(
        _peer_q_kernel,
        grid=(t // tm, ph // hps),
        in_specs=[pl.BlockSpec((tm, d), lambda i, h: (i, 0)),
                  pl.BlockSpec((None, 1, d), lambda i, h: (layer, 0, 0)),
                  pl.BlockSpec((None, 1, d), lambda i, h: (i // per_batch, 0, 0)),
                  pl.BlockSpec((None, 1, d), lambda i, h: (i // per_batch, 0, 0)),
                  pl.BlockSpec((d, hps * 2 * half), lambda i, h: (0, h)),
                  pl.BlockSpec((hps, 2, nk, half), lambda i, h: (h, 0, 0, 0))],
        out_specs=[pl.BlockSpec((tm, d), lambda i, h: (i, 0)),
                   pl.BlockSpec((hps * 2 * nk, tm), lambda i, h: (h, i))],
        out_shape=[jax.ShapeDtypeStruct((t, d), BF16),
                   jax.ShapeDtypeStruct((ph * 2 * nk, t), F32)],
        compiler_params=_params("parallel", "arbitrary"),
        name="peer_q",
    )(xt, gain.reshape(gain.shape[0], 1, d), scale, shift, wq_bf16, subkeys_bf16)


def _take_max(v, index, exact_ties):
    m = jnp.max(v, axis=0, keepdims=True)
    if exact_ties:
        first = jnp.min(jnp.where(v == m, index, float(PEER_TOPK * v.shape[0])), axis=0, keepdims=True)
        return m, index == first
    return m, v == m


def _ranked_top(v, index, exact_ties):
    rank = jnp.full(v.shape, float(PEER_TOPK), F32)
    tops = []
    for k in range(PEER_TOPK):
        m, hit = _take_max(v, index, exact_ties)
        v = jnp.where(hit, -jnp.inf, v)
        rank = jnp.where(hit, float(k), rank)
        tops.append(m)
    taken = jnp.sum(jnp.where(rank < float(PEER_TOPK), 1.0, 0.0), axis=0, keepdims=True)
    return rank, tops, taken


def _pair_counts(top1, top2, exact_ties):
    lanes = top1[0].shape[1]
    t1 = jnp.concatenate(top1, axis=0)
    a_iota = lax.broadcasted_iota(jnp.int32, (PEER_TOPK, lanes), 0).astype(F32)
    vals, idxs = [], []
    for b in range(PEER_TOPK):
        n_a = PEER_TOPK // (b + 1)
        rows = PEER_TOPK if n_a > SUBLANE else SUBLANE
        val = t1[:rows] + top2[b]
        if n_a < rows:
            val = jnp.where(a_iota[:rows] < float(n_a), val, -jnp.inf)
        vals.append(val)
        idxs.append(a_iota[:rows] * float(PEER_TOPK) + float(b))
    cand = jnp.concatenate(vals, axis=0)
    index = jnp.concatenate(idxs, axis=0)
    hits = jnp.zeros(cand.shape, F32)
    best = top1[0] + top2[0]
    z = jnp.zeros((1, lanes), F32)
    for _ in range(PEER_TOPK):
        m, hit = _take_max(cand, index, exact_ties)
        cand = jnp.where(hit, -jnp.inf, cand)
        hits = hits + jnp.where(hit, 1.0, 0.0)
        z = z + jnp.exp(m - best)
    cnt_lo = hits[0:SUBLANE]
    for b in range(1, PEER_TOPK):
        start = PEER_TOPK + (b - 1) * SUBLANE
        cnt_lo = cnt_lo + hits[start:start + SUBLANE]
    cnt = jnp.concatenate([cnt_lo, hits[SUBLANE:PEER_TOPK]], axis=0)
    return cnt, z, jnp.sum(cnt, axis=0, keepdims=True)


def _topk_strips(s_ref, c1_ref, p1_ref, r2_ref, p2_ref, exact_ties):
    nk = s_ref.shape[0] // 2
    index = lax.broadcasted_iota(jnp.int32, (nk, LANE), 0).astype(F32)
    tied = jnp.zeros((1, LANE), F32)
    for strip in range(s_ref.shape[1] // LANE):
        cols = slice(strip * LANE, (strip + 1) * LANE)
        s1 = s_ref[0:nk, cols]
        s2 = s_ref[nk:2 * nk, cols]
        rank1, top1, taken1 = _ranked_top(s1, index, exact_ties)
        rank2, top2, taken2 = _ranked_top(s2, index, exact_ties)
        cnt, z, taken = _pair_counts(top1, top2, exact_ties)
        c1 = jnp.zeros(s1.shape, F32)
        for a in range(PEER_TOPK):
            c1 = jnp.where(rank1 == float(a), cnt[a:a + 1], c1)
        c1_ref[:, cols] = c1
        p1_ref[:, cols] = jnp.exp(s1 - top1[0]) * (1.0 / z)
        r2_ref[:, cols] = rank2.astype(r2_ref.dtype)
        p2_ref[:, cols] = jnp.exp(s2 - top2[0]).astype(p2_ref.dtype)
        for n_taken in (taken1, taken2, taken):
            tied = jnp.maximum(tied, jnp.abs(n_taken - float(PEER_TOPK)))
    return tied


def _topk_kernel(s_ref, c1_ref, p1_ref, r2_ref, p2_ref):
    tied = _topk_strips(s_ref, c1_ref, p1_ref, r2_ref, p2_ref, exact_ties=False)

    @pl.when(jnp.max(tied) > 0.0)
    def _():
        _topk_strips(s_ref, c1_ref, p1_ref, r2_ref, p2_ref, exact_ties=True)


def _peer_topk(s_t, ph, nk, tk):
    t = s_t.shape[1]
    out_spec = pl.BlockSpec((nk, tk), lambda h, i: (h, i))
    return pl.pallas_call(
        _topk_kernel,
        grid=(ph, t // tk),
        in_specs=[pl.BlockSpec((2 * nk, tk), lambda h, i: (h, i))],
        out_specs=[out_spec] * 4,
        out_shape=[jax.ShapeDtypeStruct((ph * nk, t), dt) for dt in (F32, F32, BF16, BF16)],
        compiler_params=_params("parallel", "parallel"),
        name="peer_topk",
    )(s_t)


def _gelu_tanh(x):
    c = math.sqrt(2.0 / math.pi)
    z2 = x * (2.0 * c + (2.0 * c * 0.044715) * (x * x))
    return x / (1.0 + jnp.exp(-z2))


def _peer_ffn_kernel(h_ref, u_ref, vt_ref, c1_ref, p1_ref, r2_ref, p2_ref, o_ref, act_scr, bc_scr, *, nk):
    j = pl.program_id(1)
    te = u_ref.shape[0]
    tt = h_ref.shape[0]
    heads = c1_ref.shape[0]
    keys_per_step = te // nk
    steps_per_tile = SUBLANE // keys_per_step

    @pl.when(j == 0)
    def _():
        o_ref[...] = jnp.zeros_like(o_ref)

    a = lax.dot_general(u_ref[...], h_ref[...], NT_DIMS, preferred_element_type=F32)
    act_scr[...] = _gelu_tanh(a).astype(BF16)
    rows = BF16_SUBLANES
    zero = jnp.zeros((rows, tt), BF16)
    for e in range(keys_per_step):
        row = (j % steps_per_tile) * keys_per_step + e
        for h in range(heads):
            bc_scr[e, h, 0] = jnp.broadcast_to(c1_ref[h, pl.ds(row, 1), :].astype(BF16), (rows, tt))
            bc_scr[e, h, 1] = jnp.broadcast_to(p1_ref[h, pl.ds(row, 1), :].astype(BF16), (rows, tt))
        for rc in range(nk // rows):
            gate = zero
            for h in range(heads):
                r2 = r2_ref[h * nk + rc * rows:h * nk + (rc + 1) * rows, :]
                p2 = p2_ref[h * nk + rc * rows:h * nk + (rc + 1) * rows, :]
                gate = gate + jnp.where(r2 < bc_scr[e, h, 0], bc_scr[e, h, 1] * p2, zero)
            dst = slice(e * nk + rc * rows, e * nk + (rc + 1) * rows)
            act_scr[dst, :] = act_scr[dst, :] * gate
    o_ref[...] += jnp.dot(vt_ref[...], act_scr[...], preferred_element_type=F32)


def _peer_ffn(h2, u_bf16, vt_blocks, tables, heads, nk, tt):
    t, d = h2.shape
    n_chunks, _, te = vt_blocks.shape
    keys_per_step = te // nk
    assert SUBLANE % keys_per_step == 0
    steps_per_tile = SUBLANE // keys_per_step
    c1, p1, r2, p2 = tables
    row_spec = pl.BlockSpec((heads, SUBLANE, tt), lambda i, j: (0, j // steps_per_tile, i))
    tab_spec = pl.BlockSpec((heads * nk, tt), lambda i, j: (0, i))
    return pl.pallas_call(
        functools.partial(_peer_ffn_kernel, nk=nk),
        grid=(t // tt, n_chunks),
        in_specs=[pl.BlockSpec((tt, d), lambda i, j: (i, 0)),
                  pl.BlockSpec((te, d), lambda i, j: (j, 0)),
                  pl.BlockSpec((None, d, te), lambda i, j: (j, 0, 0)),
                  row_spec, row_spec, tab_spec, tab_spec],
        out_specs=pl.BlockSpec((d, tt), lambda i, j: (0, i)),
        out_shape=jax.ShapeDtypeStruct((d, t), F32),
        scratch_shapes=[pltpu.VMEM((te, tt), BF16),
                        pltpu.VMEM((keys_per_step, heads, 2, BF16_SUBLANES, tt), BF16)],
        compiler_params=_params("parallel", "arbitrary"),
        name="peer_ffn",
    )(h2, u_bf16, vt_blocks, c1.reshape(heads, nk, t), p1.reshape(heads, nk, t), r2, p2)


def _residual_kernel(x_ref, pt_ref, gt_ref, g_ref, o_ref, *, final_norm):
    x = x_ref[...] + gt_ref[...] * jnp.transpose(pt_ref[...])
    if final_norm:
        x = (x * lax.rsqrt(jnp.mean(x * x, axis=-1, keepdims=True) + RMS_EPS)) * g_ref[...]
    o_ref[...] = x


def _residual(xt, peer_t, gate, g_final, seq, tm, final_norm):
    t, d = xt.shape
    per_batch = seq // tm
    return pl.pallas_call(
        functools.partial(_residual_kernel, final_norm=final_norm),
        grid=(t // tm,),
        in_specs=[pl.BlockSpec((tm, d), lambda i: (i, 0)),
                  pl.BlockSpec((d, tm), lambda i: (0, i)),
                  pl.BlockSpec((None, 1, d), lambda i: (i // per_batch, 0, 0)),
                  pl.BlockSpec((1, d), lambda i: (0, 0))],
        out_specs=pl.BlockSpec((tm, d), lambda i: (i, 0)),
        out_shape=jax.ShapeDtypeStruct((t, d), F32),
        compiler_params=_params("parallel"),
        name="residual",
    )(xt, peer_t, gate, g_final.reshape(1, d))


def kernel(x, c, w_ada, b_ada, g_mix, w_in, rel_bias, pool_w, pool_scale, w_out, g_ffn, peer_wq, peer_subkeys,
           peer_u, peer_v, g_final):
    batch, seq, d = x.shape
    t = batch * seq
    depth = w_ada.shape[0]
    pool_total = pool_scale.shape[-1]
    att_w = (w_in.shape[-1] - pool_total) // 3
    heads = att_w // HEAD_DIM
    ph, _, nk, _ = peer_subkeys.shape[1:]
    n_exp = peer_u.shape[1]
    tm = min(512, seq)
    tn = min(1024, d)
    te = 4 * nk

    xt = x.reshape(t, d)
    bias_tables = _attention_bias_tables(rel_bias)

    for layer in range(depth):
        mod = _ada(c, w_ada, b_ada, layer)
        sh1, sc1, gt1, sh2, sc2, gt2 = [m.reshape(batch, 1, d) for m in jnp.split(mod, 6, axis=-1)]

        proj_cm = _inproj(xt, g_mix, sc1, sh1, w_in[layer].astype(BF16), layer, seq, tm, tn)
        attn_hm = _attention(proj_cm, bias_tables, batch, seq, heads)
        xt = _outproj(attn_hm, proj_cm, pool_w[layer].astype(BF16), pool_scale, w_out[layer].astype(BF16),
                      xt, gt1, layer, seq, heads, tm, tn)

        h2, s_t = _peer_q(xt, g_ffn, sc2, sh2, peer_wq[layer].astype(BF16),
                          peer_subkeys[layer].astype(BF16), layer, seq, tm)
        tables = _peer_topk(s_t, ph, nk, tk=min(512, t))
        vt_blocks = jnp.transpose(peer_v[layer].reshape(n_exp // te, te, d), (0, 2, 1)).astype(BF16)
        peer_t = _peer_ffn(h2, peer_u[layer].astype(BF16), vt_blocks, tables, ph, nk, tt=min(512, t))
        xt = _residual(xt, peer_t, gt2, g_final, seq, min(256, seq), final_norm=(layer == depth - 1))

    return xt.reshape(batch, seq, d)
```

```python
import functools
import math

import numpy as np
import jax
import jax.numpy as jnp
from jax import lax
from jax.experimental import pallas as pl
from jax.experimental.pallas import tpu as pltpu

HEAD_DIM = 128
DIL_PAIRS = ((128, 1), (512, 4), (2048, 16))
NUM_BUCKETS = 32
MAX_DISTANCE = 1024
POOL_WINDOWS = (2, 4, 8, 16)
PEER_TOPK = 16
RMS_EPS = 1e-6
NEG_INF = -1e30

LANE = 128
SUBLANE = 8
BF16_SUBLANES = 16
VMEM_LIMIT_BYTES = 58 * 1024 * 1024

ATT_SIDE = 64
ATT_TQ = 2 * ATT_SIDE
ATT_TK = ATT_TQ + 2 * ATT_SIDE
ATT_MACRO = ATT_TQ * max(d for _, d in DIL_PAIRS)
ATT_VARIANTS = 3
ATT_UNROLL = 16

F32 = jnp.float32
BF16 = jnp.bfloat16
NT_DIMS = (((1,), (1,)), ((), ()))


def _params(*semantics):
    return pltpu.CompilerParams(dimension_semantics=semantics, vmem_limit_bytes=VMEM_LIMIT_BYTES)


def _rows(start, size, stride):
    return pl.ds(start, size) if stride == 1 else pl.ds(start, size, stride=stride)


ADA_SPLIT = 4
PEER_SCORE_DOTS = 2


def _ada_kernel(ct_ref, *refs):
    w_refs, (b_ref, o_ref, acc_scr) = refs[:ADA_SPLIT], refs[ADA_SPLIT:]
    k = pl.program_id(0)
    batch = ct_ref.shape[1]
    tk, group = w_refs[0].shape

    @pl.when(k == 0)
    def _():
        acc_scr[...] = jnp.zeros_like(acc_scr)

    s = jax.nn.silu(ct_ref[...])
    s_cols = [jnp.broadcast_to(s[:, b:b + 1], (tk, LANE)) for b in range(batch)]
    for q, w_ref in enumerate(w_refs):
        for lt in range(group // LANE):
            w = w_ref[:, lt * LANE:(lt + 1) * LANE]
            cols = slice(q * group + lt * LANE, q * group + (lt + 1) * LANE)
            for b in range(batch):
                prod = w * s_cols[b]
                acc_scr[b, :, cols] += jnp.sum(prod.reshape(tk // SUBLANE, SUBLANE, LANE), axis=0)

    @pl.when(k == pl.num_programs(0) - 1)
    def _():
        o_ref[...] = jnp.zeros_like(o_ref)
        for b in range(batch):
            o_ref[b:b + 1, :] = jnp.sum(acc_scr[b], axis=0, keepdims=True) + b_ref[...]


def _ada(c, w_ada, b_ada, layer, tk=128):
    batch, d = c.shape
    n = w_ada.shape[-1]
    rows = -(-batch // SUBLANE) * SUBLANE
    return pl.pallas_call(
        _ada_kernel,
        grid=(d // tk,),
        in_specs=[pl.BlockSpec((tk, batch), lambda k: (k, 0))]
        + [pl.BlockSpec((None, tk, n // ADA_SPLIT), lambda k, q=q: (layer, k, q)) for q in range(ADA_SPLIT)]
        + [pl.BlockSpec((None, 1, n), lambda k: (layer, 0, 0))],
        out_specs=pl.BlockSpec((rows, n), lambda k: (0, 0)),
        out_shape=jax.ShapeDtypeStruct((rows, n), F32),
        scratch_shapes=[pltpu.VMEM((batch, SUBLANE, n), F32)],
        compiler_params=_params("arbitrary"),
        name="ada",
    )(jnp.transpose(c), *([w_ada] * ADA_SPLIT), b_ada.reshape(b_ada.shape[0], 1, n))[:batch]


NORM_ROWS = 16


def _modulated_rmsnorm_rows(x_ref, g_ref, sc_ref, sh_ref, o_ref):
    def body(c, carry):
        rows = pl.ds(pl.multiple_of(c * NORM_ROWS, NORM_ROWS), NORM_ROWS)
        x = x_ref[rows, :]
        y = x * lax.rsqrt(jnp.mean(x * x, axis=-1, keepdims=True) + RMS_EPS)
        o_ref[rows, :] = ((y * g_ref[...]) * (1.0 + sc_ref[...]) + sh_ref[...]).astype(o_ref.dtype)
        return carry

    lax.fori_loop(0, x_ref.shape[0] // NORM_ROWS, body, 0, unroll=4)


def _inproj_kernel(x_ref, g_ref, sc_ref, sh_ref, w_ref, o_ref, h_scr):
    @pl.when(pl.program_id(1) == 0)
    def _():
        _modulated_rmsnorm_rows(x_ref, g_ref, sc_ref, sh_ref, h_scr)

    blocks = o_ref.shape[0] // 2
    for half in range(2):
        acc = jnp.dot(h_scr[...], w_ref[:, half * blocks * LANE:(half + 1) * blocks * LANE],
                      preferred_element_type=F32)
        for k in range(blocks):
            o_ref[half * blocks + k] = acc[:, k * LANE:(k + 1) * LANE]


def _inproj(xt, gain, scale, shift, w_bf16, layer, seq, tm, tn):
    t, d = xt.shape
    n = w_bf16.shape[-1]
    per_batch = seq // tm
    return pl.pallas_call(
        _inproj_kernel,
        grid=(t // tm, n // tn),
        in_specs=[pl.BlockSpec((tm, d), lambda i, j: (i, 0)),
                  pl.BlockSpec((None, 1, d), lambda i, j: (layer, 0, 0)),
                  pl.BlockSpec((None, 1, d), lambda i, j: (i // per_batch, 0, 0)),
                  pl.BlockSpec((None, 1, d), lambda i, j: (i // per_batch, 0, 0)),
                  pl.BlockSpec((d, tn), lambda i, j: (0, j))],
        out_specs=pl.BlockSpec((tn // LANE, tm, LANE), lambda i, j: (j, i, 0)),
        out_shape=jax.ShapeDtypeStruct((n // LANE, t, LANE), F32),
        scratch_shapes=[pltpu.VMEM((tm, d), BF16)],
        compiler_params=_params("parallel", "arbitrary"),
        name="inproj",
    )(xt, gain.reshape(gain.shape[0], 1, d), scale, shift, w_bf16)


def _t5_bucket_np(rel):
    half = NUM_BUCKETS // 2
    n = -rel
    ret = np.where(n < 0, half, 0)
    n = np.abs(n)
    max_exact = half // 2
    nf = np.maximum(n, 1).astype(np.float32)
    large = max_exact + (np.log(nf / np.float32(max_exact)) / np.float32(math.log(MAX_DISTANCE / max_exact))
                         * np.float32(half - max_exact)).astype(np.int32)
    large = np.minimum(large, half - 1)
    return ret + np.where(n < max_exact, n, large)


def _attention_bucket_tables():
    i = np.arange(ATT_TQ)[:, None]
    j = np.arange(ATT_TK)[None, :]
    tables = []
    for _, dil in DIL_PAIRS:
        for off in (0, -ATT_SIDE, -2 * ATT_SIDE):
            rel = j + off - i
            tables.append(np.where(np.abs(rel) <= ATT_SIDE, _t5_bucket_np(rel * dil), -1))
    return np.stack(tables).astype(np.int32)


def _bias_kernel(rb_ref, bucket_ref, o_ref):
    h = pl.program_id(0)
    bucket = bucket_ref[...]
    out = jnp.full(bucket.shape, NEG_INF, F32)
    for b in range(NUM_BUCKETS):
        out = jnp.where(bucket == b, rb_ref[b, h], out)
    o_ref[...] = out


def _attention_bias_tables(rel_bias):
    buckets = jnp.asarray(_attention_bucket_tables())
    heads = rel_bias.shape[1]
    return pl.pallas_call(
        _bias_kernel,
        grid=(heads,),
        in_specs=[pl.BlockSpec(memory_space=pltpu.SMEM),
                  pl.BlockSpec(buckets.shape, lambda h: (0, 0, 0))],
        out_specs=pl.BlockSpec((None,) + buckets.shape, lambda h: (h, 0, 0, 0)),
        out_shape=jax.ShapeDtypeStruct((heads,) + buckets.shape, F32),
        compiler_params=_params("arbitrary"),
        name="attn_bias",
    )(rel_bias, buckets)


def _attn_kernel(q_ref, k_ref, v_ref, bias_ref, o_ref, ob_scr, lse_scr, *, seq):
    scale = HEAD_DIM ** -0.5

    def macro_tile(t, carry):
        p0 = pl.multiple_of(t * ATT_MACRO, ATT_MACRO)
        for g, (_, dil) in enumerate(DIL_PAIRS):
            length = seq // dil
            nblk = ATT_MACRO // (dil * ATT_TQ)

            def block(it, g=g, dil=dil, length=length):
                r = it % dil
                n = it // dil
                q0 = p0 // dil + n * ATT_TQ
                k0 = jnp.clip(q0 - ATT_SIDE, 0, length - ATT_TK)
                variant = jnp.where(q0 == 0, 0, jnp.where(q0 == length - ATT_TQ, 2, 1))
                q = q_ref[_rows(r + q0 * dil, ATT_TQ, dil), :].astype(BF16)
                k = k_ref[_rows(r + k0 * dil, ATT_TK, dil), :].astype(BF16)
                v = v_ref[_rows(r + k0 * dil, ATT_TK, dil), :].astype(BF16)
                s = lax.dot_general(q, k, NT_DIMS, preferred_element_type=F32)
                s = s * scale + bias_ref[g * ATT_VARIANTS + variant]
                m = jnp.max(s, axis=1, keepdims=True)
                p = jnp.exp(s - m)
                l = jnp.sum(p, axis=1, keepdims=True)
                o = jnp.dot(p.astype(BF16), v, preferred_element_type=F32) / l
                lse = m + jnp.log(l)
                dst = _rows(r + n * ATT_TQ * dil, ATT_TQ, dil)
                ob_scr[g, dst, :] = o
                lse_scr[g, dst, :] = jnp.broadcast_to(lse, (ATT_TQ, LANE))

            def block_group(gi, c, block=block):
                for u in range(ATT_UNROLL):
                    block(gi * ATT_UNROLL + u)
                return c

            lax.fori_loop(0, dil * nblk // ATT_UNROLL, block_group, 0)

        chunk = 256

        def merge(ci, c):
            rows = pl.ds(pl.multiple_of(ci * chunk, chunk), chunk)
            l0, l1, l2 = lse_scr[0, rows, :], lse_scr[1, rows, :], lse_scr[2, rows, :]
            mx = jnp.maximum(jnp.maximum(l0, l1), l2)
            e0, e1, e2 = jnp.exp(l0 - mx), jnp.exp(l1 - mx), jnp.exp(l2 - mx)
            den = e0 + e1 + e2
            out = (e0 / den) * ob_scr[0, rows, :] + (e1 / den) * ob_scr[1, rows, :] + (e2 / den) * ob_scr[2, rows, :]
            o_ref[pl.ds(p0 + pl.multiple_of(ci * chunk, chunk), chunk), :] = out.astype(o_ref.dtype)
            return c

        lax.fori_loop(0, ATT_MACRO // chunk, merge, 0)
        return carry

    lax.fori_loop(0, seq // ATT_MACRO, macro_tile, 0)


def _attention(proj_cm, bias_tables, batch, seq, heads):
    t = proj_cm.shape[1]
    assert seq % ATT_MACRO == 0 and seq // max(d for _, d in DIL_PAIRS) >= ATT_TK
    nb = len(DIL_PAIRS)

    def qkv_spec(off):
        return pl.BlockSpec((None, seq, LANE), lambda b, h, off=off: (off + h, b, 0))

    return pl.pallas_call(
        functools.partial(_attn_kernel, seq=seq),
        grid=(batch, heads),
        in_specs=[qkv_spec(0), qkv_spec(heads), qkv_spec(2 * heads),
                  pl.BlockSpec((None, nb * ATT_VARIANTS, ATT_TQ, ATT_TK), lambda b, h: (h, 0, 0, 0))],
        out_specs=pl.BlockSpec((None, seq, LANE), lambda b, h: (h, b, 0)),
        out_shape=jax.ShapeDtypeStruct((heads, t, LANE), BF16),
        scratch_shapes=[pltpu.VMEM((nb, ATT_MACRO, LANE), F32),
                        pltpu.VMEM((nb, ATT_MACRO, LANE), F32)],
        compiler_params=_params("parallel", "arbitrary"),
        name="attn",
    )(proj_cm, proj_cm, proj_cm, bias_tables)


def _outproj_kernel(attn_ref, p_ref, prev_ref, next_ref, pw_ref, ps_ref, w_ref, x_ref, gt_ref, o_ref,
                    lhs_scr, pooled_scr, ext_scr, *, seq, tm):
    i = pl.program_id(0)
    heads = attn_ref.shape[0]
    att_w = heads * LANE
    n_pool_blocks = p_ref.shape[0]
    pool_group = pw_ref.shape[-1]

    @pl.when(pl.program_id(1) == 0)
    def _():
        for h in range(heads):
            lhs_scr[:, h * LANE:(h + 1) * LANE] = attn_ref[h]

        base = (i * tm) % seq
        has_prev = (base != 0).astype(F32)
        has_next = (base + tm != seq).astype(F32)
        pos = base + lax.broadcasted_iota(jnp.int32, (tm, LANE), 0)
        for cb in range(n_pool_blocks):
            w = POOL_WINDOWS[(cb * LANE) // pool_group]
            main = p_ref[cb]
            ext_scr[0:SUBLANE, :] = prev_ref[cb] * has_prev
            ext_scr[SUBLANE:SUBLANE + tm, :] = main
            ext_scr[SUBLANE + tm:2 * SUBLANE + tm, :] = next_ref[cb] * has_next
            total = ext_scr[pl.ds(SUBLANE - w // 2, tm), :]
            for off in range(-w // 2 + 1, w // 2):
                total = total + ext_scr[pl.ds(SUBLANE + off, tm), :]
            count = (jnp.minimum(pos + w // 2, seq) - jnp.maximum(pos - w // 2, 0)).astype(F32)
            pooled_scr[:, cb * LANE:(cb + 1) * LANE] = (total / count - main).astype(BF16)
        for gi in range(len(POOL_WINDOWS)):
            cols = slice(gi * pool_group, (gi + 1) * pool_group)
            mixed = jnp.dot(pooled_scr[:, cols], pw_ref[gi], preferred_element_type=F32)
            lhs_scr[:, att_w + gi * pool_group:att_w + (gi + 1) * pool_group] = (mixed * ps_ref[:, cols]).astype(BF16)

    half_w = o_ref.shape[1] // 2
    for half in range(2):
        cols = slice(half * half_w, (half + 1) * half_w)
        mix = jnp.dot(lhs_scr[...], w_ref[:, cols], preferred_element_type=F32)
        o_ref[:, cols] = x_ref[:, cols] + gt_ref[:, cols] * mix


def _outproj(attn_hm, proj_cm, pool_w_bf16, pool_scale, w_out_bf16, xt, gate, layer, seq, heads, tm, tn):
    t, d = xt.shape
    pool_w_total = pool_scale.shape[-1]
    npb = pool_w_total // LANE
    pool_blk = (3 * heads) // npb
    assert pool_blk * npb == 3 * heads and tm % SUBLANE == 0
    per_batch = seq // tm
    halo_per_tile = tm // SUBLANE
    n_halo = t // SUBLANE
    mix_w = w_out_bf16.shape[0]
    return pl.pallas_call(
        functools.partial(_outproj_kernel, seq=seq, tm=tm),
        grid=(t // tm, d // tn),
        in_specs=[pl.BlockSpec((heads, tm, LANE), lambda i, j: (0, i, 0)),
                  pl.BlockSpec((npb, tm, LANE), lambda i, j: (pool_blk, i, 0)),
                  pl.BlockSpec((npb, SUBLANE, LANE),
                               lambda i, j: (pool_blk, jnp.maximum(i * halo_per_tile - 1, 0), 0)),
                  pl.BlockSpec((npb, SUBLANE, LANE),
                               lambda i, j: (pool_blk, jnp.minimum((i + 1) * halo_per_tile, n_halo - 1), 0)),
                  pl.BlockSpec(pool_w_bf16.shape, lambda i, j: (0, 0, 0)),
                  pl.BlockSpec((None, 1, pool_w_total), lambda i, j: (layer, 0, 0)),
                  pl.BlockSpec((mix_w, tn), lambda i, j: (0, j)),
                  pl.BlockSpec((tm, tn), lambda i, j: (i, j)),
                  pl.BlockSpec((None, 1, tn), lambda i, j: (i // per_batch, 0, j))],
        out_specs=pl.BlockSpec((tm, tn), lambda i, j: (i, j)),
        out_shape=jax.ShapeDtypeStruct((t, d), F32),
        scratch_shapes=[pltpu.VMEM((tm, mix_w), BF16),
                        pltpu.VMEM((tm, pool_w_total), BF16),
                        pltpu.VMEM((tm + 2 * SUBLANE, LANE), F32)],
        compiler_params=_params("parallel", "arbitrary"),
        name="outproj",
    )(attn_hm, proj_cm, proj_cm, proj_cm, pool_w_bf16,
      pool_scale.reshape(pool_scale.shape[0], 1, pool_w_total), w_out_bf16, xt, gate)


def _peer_q_kernel(x_ref, g_ref, sc_ref, sh_ref, wq_ref, sk_ref, h_ref, s_ref):
    @pl.when(pl.program_id(1) == 0)
    def _():
        _modulated_rmsnorm_rows(x_ref, g_ref, sc_ref, sh_ref, h_ref)

    q = jnp.dot(h_ref[...], wq_ref[...], preferred_element_type=F32).astype(BF16)
    hps, _, nk, half = sk_ref.shape
    for hh in range(hps):
        for p in range(2):
            blk = hh * 2 + p
            s_ref[blk * nk:(blk + 1) * nk, :] = lax.dot_general(
                sk_ref[hh, p], q[:, blk * half:(blk + 1) * half], NT_DIMS, preferred_element_type=F32)


def _peer_q(xt, gain, scale, shift, wq_bf16, subkeys_bf16, layer, seq, tm):
    t, d = xt.shape
    ph, _, nk, half = subkeys_bf16.shape
    per_batch = seq // tm
    hps = 2 if ph % 2 == 0 else 1
    return pl.pallas_call(
        _peer_q_kernel,
        grid=(t // tm, ph // hps),
        in_specs=[pl.BlockSpec((tm, d), lambda i, h: (i, 0)),
                  pl.BlockSpec((None, 1, d), lambda i, h: (layer, 0, 0)),
                  pl.BlockSpec((None, 1, d), lambda i, h: (i // per_batch, 0, 0)),
                  pl.BlockSpec((None, 1, d), lambda i, h: (i // per_batch, 0, 0)),
                  pl.BlockSpec((d, hps * 2 * half), lambda i, h: (0, h)),
                  pl.BlockSpec((hps, 2, nk, half), lambda i, h: (h, 0, 0, 0))],
        out_specs=[pl.BlockSpec((tm, d), lambda i, h: (i, 0)),
                   pl.BlockSpec((hps * 2 * nk, tm), lambda i, h: (h, i))],
        out_shape=[jax.ShapeDtypeStruct((t, d), BF16),
                   jax.ShapeDtypeStruct((ph * 2 * nk, t), F32)],
        compiler_params=_params("parallel", "arbitrary"),
        name="peer_q",
    )(xt, gain.reshape(gain.shape[0], 1, d), scale, shift, wq_bf16, subkeys_bf16)


def _take_max(v, index, exact_ties):
    m = jnp.max(v, axis=0, keepdims=True)
    if exact_ties:
        first = jnp.min(jnp.where(v == m, index, float(PEER_TOPK * v.shape[0])), axis=0, keepdims=True)
        return m, index == first
    return m, v == m


def _ranked_top(v, index, exact_ties):
    rank = jnp.full(v.shape, float(PEER_TOPK), F32)
    tops = []
    for k in range(PEER_TOPK):
        m, hit = _take_max(v, index, exact_ties)
        v = jnp.where(hit, -jnp.inf, v)
        rank = jnp.where(hit, float(k), rank)
        tops.append(m)
    taken = jnp.sum(jnp.where(rank < float(PEER_TOPK), 1.0, 0.0), axis=0, keepdims=True)
    return rank, tops, taken


def _pair_counts(top1, top2, exact_ties):
    lanes = top1[0].shape[1]
    t1 = jnp.concatenate(top1, axis=0)
    a_iota = lax.broadcasted_iota(jnp.int32, (PEER_TOPK, lanes), 0).astype(F32)
    vals, idxs = [], []
    for b in range(PEER_TOPK):
        n_a = PEER_TOPK // (b + 1)
        rows = PEER_TOPK if n_a > SUBLANE else SUBLANE
        val = t1[:rows] + top2[b]
        if n_a < rows:
            val = jnp.where(a_iota[:rows] < float(n_a), val, -jnp.inf)
        vals.append(val)
        idxs.append(a_iota[:rows] * float(PEER_TOPK) + float(b))
    cand = jnp.concatenate(vals, axis=0)
    index = jnp.concatenate(idxs, axis=0)
    hits = jnp.zeros(cand.shape, F32)
    best = top1[0] + top2[0]
    z = jnp.zeros((1, lanes), F32)
    for _ in range(PEER_TOPK):
        m, hit = _take_max(cand, index, exact_ties)
        cand = jnp.where(hit, -jnp.inf, cand)
        hits = hits + jnp.where(hit, 1.0, 0.0)
        z = z + jnp.exp(m - best)
    cnt_lo = hits[0:SUBLANE]
    for b in range(1, PEER_TOPK):
        start = PEER_TOPK + (b - 1) * SUBLANE
        cnt_lo = cnt_lo + hits[start:start + SUBLANE]
    cnt = jnp.concatenate([cnt_lo, hits[SUBLANE:PEER_TOPK]], axis=0)
    return cnt, z, jnp.sum(cnt, axis=0, keepdims=True)


def _topk_strips(s_ref, c1_ref, p1_ref, r2_ref, p2_ref, exact_ties):
    nk = s_ref.shape[0] // 2
    index = lax.broadcasted_iota(jnp.int32, (nk, LANE), 0).astype(F32)
    tied = jnp.zeros((1, LANE), F32)
    for strip in range(s_ref.shape[1] // LANE):
        cols = slice(strip * LANE, (strip + 1) * LANE)
        s1 = s_ref[0:nk, cols]
        s2 = s_ref[nk:2 * nk, cols]
        rank1, top1, taken1 = _ranked_top(s1, index, exact_ties)
        rank2, top2, taken2 = _ranked_top(s2, index, exact_ties)
        cnt, z, taken = _pair_counts(top1, top2, exact_ties)
        c1 = jnp.zeros(s1.shape, F32)
        for a in range(PEER_TOPK):
            c1 = jnp.where(rank1 == float(a), cnt[a:a + 1], c1)
        c1_ref[:, cols] = c1
        p1_ref[:, cols] = jnp.exp(s1 - top1[0]) * (1.0 / z)
        r2_ref[:, cols] = rank2.astype(r2_ref.dtype)
        p2_ref[:, cols] = jnp.exp(s2 - top2[0]).astype(p2_ref.dtype)
        for n_taken in (taken1, taken2, taken):
            tied = jnp.maximum(tied, jnp.abs(n_taken - float(PEER_TOPK)))
    return tied


def _topk_kernel(s_ref, c1_ref, p1_ref, r2_ref, p2_ref):
    tied = _topk_strips(s_ref, c1_ref, p1_ref, r2_ref, p2_ref, exact_ties=False)

    @pl.when(jnp.max(tied) > 0.0)
    def _():
        _topk_strips(s_ref, c1_ref, p1_ref, r2_ref, p2_ref, exact_ties=True)


def _peer_topk(s_t, ph, nk, tk):
    t = s_t.shape[1]
    out_spec = pl.BlockSpec((nk, tk), lambda h, i: (h, i))
    return pl.pallas_call(
        _topk_kernel,
        grid=(ph, t // tk),
        in_specs=[pl.BlockSpec((2 * nk, tk), lambda h, i: (h, i))],
        out_specs=[out_spec] * 4,
        out_shape=[jax.ShapeDtypeStruct((ph * nk, t), dt) for dt in (F32, F32, BF16, BF16)],
        compiler_params=_params("parallel", "parallel"),
        name="peer_topk",
    )(s_t)


def _gelu_tanh(x):
    c = math.sqrt(2.0 / math.pi)
    z2 = x * (2.0 * c + (2.0 * c * 0.044715) * (x * x))
    return x / (1.0 + jnp.exp(-z2))


def _peer_ffn_kernel(h_ref, u_ref, vt_ref, c1_ref, p1_ref, r2_ref, p2_ref, o_ref, act_scr, bc_scr, *, nk):
    j = pl.program_id(1)
    te = u_ref.shape[0]
    tt = h_ref.shape[0]
    heads = c1_ref.shape[0]
    keys_per_step = te // nk
    steps_per_tile = SUBLANE // keys_per_step

    @pl.when(j == 0)
    def _():
        o_ref[...] = jnp.zeros_like(o_ref)

    rows = BF16_SUBLANES
    zero = jnp.zeros((rows, tt), BF16)
    keys_per_dot = max(keys_per_step // PEER_SCORE_DOTS, 1)
    for e in range(keys_per_step):
        if e % keys_per_dot == 0:
            a = lax.dot_general(u_ref[e * nk:(e + keys_per_dot) * nk, :], h_ref[...], NT_DIMS,
                                preferred_element_type=F32)
        row = (j % steps_per_tile) * keys_per_step + e
        for h in range(heads):
            bc_scr[e, h, 0] = jnp.broadcast_to(c1_ref[h, pl.ds(row, 1), :].astype(BF16), (rows, tt))
            bc_scr[e, h, 1] = jnp.broadcast_to(p1_ref[h, pl.ds(row, 1), :].astype(BF16), (rows, tt))
        for rc in range(nk // rows):
            gate = zero
            for h in range(heads):
                r2 = r2_ref[h * nk + rc * rows:h * nk + (rc + 1) * rows, :]
                p2 = p2_ref[h * nk + rc * rows:h * nk + (rc + 1) * rows, :]
                gate = gate + jnp.where(r2 < bc_scr[e, h, 0], bc_scr[e, h, 1] * p2, zero)
            dst = slice(e * nk + rc * rows, e * nk + (rc + 1) * rows)
            src = slice((e % keys_per_dot) * nk + rc * rows, (e % keys_per_dot) * nk + (rc + 1) * rows)
            act_scr[dst, :] = _gelu_tanh(a[src, :]).astype(BF16) * gate
    o_ref[...] += jnp.dot(vt_ref[...], act_scr[...], preferred_element_type=F32)


def _peer_ffn(h2, u_bf16, vt_blocks, tables, heads, nk, tt):
    t, d = h2.shape
    n_chunks, _, te = vt_blocks.shape
    keys_per_step = te // nk
    assert SUBLANE % keys_per_step == 0
    steps_per_tile = SUBLANE // keys_per_step
    c1, p1, r2, p2 = tables
    row_spec = pl.BlockSpec((heads, SUBLANE, tt), lambda i, j: (0, j // steps_per_tile, i))
    once = pl.Buffered(1)
    tab_spec = pl.BlockSpec((heads * nk, tt), lambda i, j: (0, i), pipeline_mode=once)
    return pl.pallas_call(
        functools.partial(_peer_ffn_kernel, nk=nk),
        grid=(t // tt, n_chunks),
        in_specs=[pl.BlockSpec((tt, d), lambda i, j: (i, 0), pipeline_mode=once),
                  pl.BlockSpec((te, d), lambda i, j: (j, 0)),
                  pl.BlockSpec((None, d, te), lambda i, j: (j, 0, 0)),
                  row_spec, row_spec, tab_spec, tab_spec],
        out_specs=pl.BlockSpec((d, tt), lambda i, j: (0, i), pipeline_mode=once),
        out_shape=jax.ShapeDtypeStruct((d, t), F32),
        scratch_shapes=[pltpu.VMEM((te, tt), BF16),
                        pltpu.VMEM((keys_per_step, heads, 2, BF16_SUBLANES, tt), BF16)],
        compiler_params=_params("parallel", "arbitrary"),
        name="peer_ffn",
    )(h2, u_bf16, vt_blocks, c1.reshape(heads, nk, t), p1.reshape(heads, nk, t), r2, p2)


def _residual_kernel(x_ref, pt_ref, gt_ref, g_ref, o_ref, *, final_norm):
    x = x_ref[...] + gt_ref[...] * jnp.transpose(pt_ref[...])
    if final_norm:
        x = (x * lax.rsqrt(jnp.mean(x * x, axis=-1, keepdims=True) + RMS_EPS)) * g_ref[...]
    o_ref[...] = x


def _residual(xt, peer_t, gate, g_final, seq, tm, final_norm):
    t, d = xt.shape
    per_batch = seq // tm
    return pl.pallas_call(
        functools.partial(_residual_kernel, final_norm=final_norm),
        grid=(t // tm,),
        in_specs=[pl.BlockSpec((tm, d), lambda i: (i, 0)),
                  pl.BlockSpec((d, tm), lambda i: (0, i)),
                  pl.BlockSpec((None, 1, d), lambda i: (i // per_batch, 0, 0)),
                  pl.BlockSpec((1, d), lambda i: (0, 0))],
        out_specs=pl.BlockSpec((tm, d), lambda i: (i, 0)),
        out_shape=jax.ShapeDtypeStruct((t, d), F32),
        compiler_params=_params("parallel"),
        name="residual",
    )(xt, peer_t, gate, g_final.reshape(1, d))


def kernel(x, c, w_ada, b_ada, g_mix, w_in, rel_bias, pool_w, pool_scale, w_out, g_ffn, peer_wq, peer_subkeys,
           peer_u, peer_v, g_final):
    batch, seq, d = x.shape
    t = batch * seq
    depth = w_ada.shape[0]
    pool_total = pool_scale.shape[-1]
    att_w = (w_in.shape[-1] - pool_total) // 3
    heads = att_w // HEAD_DIM
    ph, _, nk, _ = peer_subkeys.shape[1:]
    n_exp = peer_u.shape[1]
    tm = min(512, seq)
    tn = min(1024, d)
    te = 8 * nk

    xt = x.reshape(t, d)
    bias_tables = _attention_bias_tables(rel_bias)

    for layer in range(depth):
        mod = _ada(c, w_ada, b_ada, layer)
        sh1, sc1, gt1, sh2, sc2, gt2 = [m.reshape(batch, 1, d) for m in jnp.split(mod, 6, axis=-1)]

        proj_cm = _inproj(xt, g_mix, sc1, sh1, w_in[layer].astype(BF16), layer, seq, tm, tn)
        attn_hm = _attention(proj_cm, bias_tables, batch, seq, heads)
        xt = _outproj(attn_hm, proj_cm, pool_w[layer].astype(BF16), pool_scale, w_out[layer].astype(BF16),
                      xt, gt1, layer, seq, heads, tm, tn)

        h2, s_t = _peer_q(xt, g_ffn, sc2, sh2, peer_wq[layer].astype(BF16),
                          peer_subkeys[layer].astype(BF16), layer, seq, tm)
        tables = _peer_topk(s_t, ph, nk, tk=min(512, t))
        vt_blocks = jnp.transpose(peer_v[layer].reshape(n_exp // te, te, d), (0, 2, 1)).astype(BF16)
        peer_t = _peer_ffn(h2, peer_u[layer].astype(BF16), vt_blocks, tables, ph, nk, tt=min(512, t))
        xt = _residual(xt, peer_t, gt2, g_final, seq, min(256, seq), final_norm=(layer == depth - 1))

    return xt.reshape(batch, seq, d)
```

```python
import functools
import math

import numpy as np
import jax
import jax.numpy as jnp
from jax import lax
from jax.experimental import pallas as pl
from jax.experimental.pallas import tpu as pltpu

HEAD_DIM = 128
DIL_PAIRS = ((128, 1), (512, 4), (2048, 16))
NUM_BUCKETS = 32
MAX_DISTANCE = 1024
POOL_WINDOWS = (2, 4, 8, 16)
PEER_TOPK = 16
RMS_EPS = 1e-6
NEG_INF = -1e30

LANE = 128
SUBLANE = 8
BF16_SUBLANES = 16
VMEM_LIMIT_BYTES = 58 * 1024 * 1024

ATT_SIDE = 64
ATT_TQ = 2 * ATT_SIDE
ATT_TK = ATT_TQ + 2 * ATT_SIDE
ATT_MACRO = ATT_TQ * max(d for _, d in DIL_PAIRS)
ATT_VARIANTS = 3
ATT_UNROLL = 16

F32 = jnp.float32
BF16 = jnp.bfloat16
NT_DIMS = (((1,), (1,)), ((), ()))


def _params(*semantics):
    return pltpu.CompilerParams(dimension_semantics=semantics, vmem_limit_bytes=VMEM_LIMIT_BYTES)


def _rows(start, size, stride):
    return pl.ds(start, size) if stride == 1 else pl.ds(start, size, stride=stride)


ADA_SPLIT = 4
PEER_SCORE_DOTS = 1


def _ada_kernel(ct_ref, *refs):
    w_refs, (b_ref, o_ref, acc_scr) = refs[:ADA_SPLIT], refs[ADA_SPLIT:]
    k = pl.program_id(0)
    batch = ct_ref.shape[1]
    tk, group = w_refs[0].shape

    @pl.when(k == 0)
    def _():
        acc_scr[...] = jnp.zeros_like(acc_scr)

    s = jax.nn.silu(ct_ref[...])
    s_cols = [jnp.broadcast_to(s[:, b:b + 1], (tk, LANE)) for b in range(batch)]
    for q, w_ref in enumerate(w_refs):
        for lt in range(group // LANE):
            w = w_ref[:, lt * LANE:(lt + 1) * LANE]
            cols = slice(q * group + lt * LANE, q * group + (lt + 1) * LANE)
            for b in range(batch):
                prod = w * s_cols[b]
                acc_scr[b, :, cols] += jnp.sum(prod.reshape(tk // SUBLANE, SUBLANE, LANE), axis=0)

    @pl.when(k == pl.num_programs(0) - 1)
    def _():
        o_ref[...] = jnp.zeros_like(o_ref)
        for b in range(batch):
            o_ref[b:b + 1, :] = jnp.sum(acc_scr[b], axis=0, keepdims=True) + b_ref[...]


def _ada(c, w_ada, b_ada, layer, tk=128):
    batch, d = c.shape
    n = w_ada.shape[-1]
    rows = -(-batch // SUBLANE) * SUBLANE
    return pl.pallas_call(
        _ada_kernel,
        grid=(d // tk,),
        in_specs=[pl.BlockSpec((tk, batch), lambda k: (k, 0))]
        + [pl.BlockSpec((None, tk, n // ADA_SPLIT), lambda k, q=q: (layer, k, q)) for q in range(ADA_SPLIT)]
        + [pl.BlockSpec((None, 1, n), lambda k: (layer, 0, 0))],
        out_specs=pl.BlockSpec((rows, n), lambda k: (0, 0)),
        out_shape=jax.ShapeDtypeStruct((rows, n), F32),
        scratch_shapes=[pltpu.VMEM((batch, SUBLANE, n), F32)],
        compiler_params=_params("arbitrary"),
        name="ada",
    )(jnp.transpose(c), *([w_ada] * ADA_SPLIT), b_ada.reshape(b_ada.shape[0], 1, n))[:batch]


NORM_ROWS = 16


def _modulated_rmsnorm_rows(x_ref, g_ref, sc_ref, sh_ref, o_ref):
    def body(c, carry):
        rows = pl.ds(pl.multiple_of(c * NORM_ROWS, NORM_ROWS), NORM_ROWS)
        x = x_ref[rows, :]
        y = x * lax.rsqrt(jnp.mean(x * x, axis=-1, keepdims=True) + RMS_EPS)
        o_ref[rows, :] = ((y * g_ref[...]) * (1.0 + sc_ref[...]) + sh_ref[...]).astype(o_ref.dtype)
        return carry

    lax.fori_loop(0, x_ref.shape[0] // NORM_ROWS, body, 0, unroll=4)


def _inproj_kernel(x_ref, g_ref, sc_ref, sh_ref, w_ref, o_ref, h_scr):
    @pl.when(pl.program_id(1) == 0)
    def _():
        _modulated_rmsnorm_rows(x_ref, g_ref, sc_ref, sh_ref, h_scr)

    blocks = o_ref.shape[0] // 2
    for half in range(2):
        acc = jnp.dot(h_scr[...], w_ref[:, half * blocks * LANE:(half + 1) * blocks * LANE],
                      preferred_element_type=F32)
        for k in range(blocks):
            o_ref[half * blocks + k] = acc[:, k * LANE:(k + 1) * LANE]


def _inproj(xt, gain, scale, shift, w_bf16, layer, seq, tm, tn):
    t, d = xt.shape
    n = w_bf16.shape[-1]
    per_batch = seq // tm
    return pl.pallas_call(
        _inproj_kernel,
        grid=(t // tm, n // tn),
        in_specs=[pl.BlockSpec((tm, d), lambda i, j: (i, 0)),
                  pl.BlockSpec((None, 1, d), lambda i, j: (layer, 0, 0)),
                  pl.BlockSpec((None, 1, d), lambda i, j: (i // per_batch, 0, 0)),
                  pl.BlockSpec((None, 1, d), lambda i, j: (i // per_batch, 0, 0)),
                  pl.BlockSpec((d, tn), lambda i, j: (0, j))],
        out_specs=pl.BlockSpec((tn // LANE, tm, LANE), lambda i, j: (j, i, 0)),
        out_shape=jax.ShapeDtypeStruct((n // LANE, t, LANE), F32),
        scratch_shapes=[pltpu.VMEM((tm, d), BF16)],
        compiler_params=_params("parallel", "arbitrary"),
        name="inproj",
    )(xt, gain.reshape(gain.shape[0], 1, d), scale, shift, w_bf16)


def _t5_bucket_np(rel):
    half = NUM_BUCKETS // 2
    n = -rel
    ret = np.where(n < 0, half, 0)
    n = np.abs(n)
    max_exact = half // 2
    nf = np.maximum(n, 1).astype(np.float32)
    large = max_exact + (np.log(nf / np.float32(max_exact)) / np.float32(math.log(MAX_DISTANCE / max_exact))
                         * np.float32(half - max_exact)).astype(np.int32)
    large = np.minimum(large, half - 1)
    return ret + np.where(n < max_exact, n, large)


def _attention_bucket_tables():
    i = np.arange(ATT_TQ)[:, None]
    j = np.arange(ATT_TK)[None, :]
    tables = []
    for _, dil in DIL_PAIRS:
        for off in (0, -ATT_SIDE, -2 * ATT_SIDE):
            rel = j + off - i
            tables.append(np.where(np.abs(rel) <= ATT_SIDE, _t5_bucket_np(rel * dil), -1))
    return np.stack(tables).astype(np.int32)


def _bias_kernel(rb_ref, bucket_ref, o_ref):
    h = pl.program_id(0)
    bucket = bucket_ref[...]
    out = jnp.full(bucket.shape, NEG_INF, F32)
    for b in range(NUM_BUCKETS):
        out = jnp.where(bucket == b, rb_ref[b, h], out)
    o_ref[...] = out


def _attention_bias_tables(rel_bias):
    buckets = jnp.asarray(_attention_bucket_tables())
    heads = rel_bias.shape[1]
    return pl.pallas_call(
        _bias_kernel,
        grid=(heads,),
        in_specs=[pl.BlockSpec(memory_space=pltpu.SMEM),
                  pl.BlockSpec(buckets.shape, lambda h: (0, 0, 0))],
        out_specs=pl.BlockSpec((None,) + buckets.shape, lambda h: (h, 0, 0, 0)),
        out_shape=jax.ShapeDtypeStruct((heads,) + buckets.shape, F32),
        compiler_params=_params("arbitrary"),
        name="attn_bias",
    )(rel_bias, buckets)


def _attn_kernel(q_ref, k_ref, v_ref, bias_ref, o_ref, ob_scr, lse_scr, *, seq):
    scale = HEAD_DIM ** -0.5

    def macro_tile(t, carry):
        p0 = pl.multiple_of(t * ATT_MACRO, ATT_MACRO)
        for g, (_, dil) in enumerate(DIL_PAIRS):
            length = seq // dil
            nblk = ATT_MACRO // (dil * ATT_TQ)

            def block(it, g=g, dil=dil, length=length):
                r = it % dil
                n = it // dil
                q0 = p0 // dil + n * ATT_TQ
                k0 = jnp.clip(q0 - ATT_SIDE, 0, length - ATT_TK)
                variant = jnp.where(q0 == 0, 0, jnp.where(q0 == length - ATT_TQ, 2, 1))
                q = q_ref[_rows(r + q0 * dil, ATT_TQ, dil), :].astype(BF16)
                k = k_ref[_rows(r + k0 * dil, ATT_TK, dil), :].astype(BF16)
                v = v_ref[_rows(r + k0 * dil, ATT_TK, dil), :].astype(BF16)
                s = lax.dot_general(q, k, NT_DIMS, preferred_element_type=F32)
                s = s * scale + bias_ref[g * ATT_VARIANTS + variant]
                m = jnp.max(s, axis=1, keepdims=True)
                p = jnp.exp(s - m)
                l = jnp.sum(p, axis=1, keepdims=True)
                o = jnp.dot(p.astype(BF16), v, preferred_element_type=F32) / l
                lse = m + jnp.log(l)
                dst = _rows(r + n * ATT_TQ * dil, ATT_TQ, dil)
                ob_scr[g, dst, :] = o
                lse_scr[g, dst, :] = jnp.broadcast_to(lse, (ATT_TQ, LANE))

            def block_group(gi, c, block=block):
                for u in range(ATT_UNROLL):
                    block(gi * ATT_UNROLL + u)
                return c

            lax.fori_loop(0, dil * nblk // ATT_UNROLL, block_group, 0)

        chunk = 256

        def merge(ci, c):
            rows = pl.ds(pl.multiple_of(ci * chunk, chunk), chunk)
            l0, l1, l2 = lse_scr[0, rows, :], lse_scr[1, rows, :], lse_scr[2, rows, :]
            mx = jnp.maximum(jnp.maximum(l0, l1), l2)
            e0, e1, e2 = jnp.exp(l0 - mx), jnp.exp(l1 - mx), jnp.exp(l2 - mx)
            den = e0 + e1 + e2
            out = (e0 / den) * ob_scr[0, rows, :] + (e1 / den) * ob_scr[1, rows, :] + (e2 / den) * ob_scr[2, rows, :]
            o_ref[pl.ds(p0 + pl.multiple_of(ci * chunk, chunk), chunk), :] = out.astype(o_ref.dtype)
            return c

        lax.fori_loop(0, ATT_MACRO // chunk, merge, 0)
        return carry

    lax.fori_loop(0, seq // ATT_MACRO, macro_tile, 0)


def _attention(proj_cm, bias_tables, batch, seq, heads):
    t = proj_cm.shape[1]
    assert seq % ATT_MACRO == 0 and seq // max(d for _, d in DIL_PAIRS) >= ATT_TK
    nb = len(DIL_PAIRS)

    def qkv_spec(off):
        return pl.BlockSpec((None, seq, LANE), lambda b, h, off=off: (off + h, b, 0))

    return pl.pallas_call(
        functools.partial(_attn_kernel, seq=seq),
        grid=(batch, heads),
        in_specs=[qkv_spec(0), qkv_spec(heads), qkv_spec(2 * heads),
                  pl.BlockSpec((None, nb * ATT_VARIANTS, ATT_TQ, ATT_TK), lambda b, h: (h, 0, 0, 0))],
        out_specs=pl.BlockSpec((None, seq, LANE), lambda b, h: (h, b, 0)),
        out_shape=jax.ShapeDtypeStruct((heads, t, LANE), BF16),
        scratch_shapes=[pltpu.VMEM((nb, ATT_MACRO, LANE), F32),
                        pltpu.VMEM((nb, ATT_MACRO, LANE), F32)],
        compiler_params=_params("parallel", "arbitrary"),
        name="attn",
    )(proj_cm, proj_cm, proj_cm, bias_tables)


def _outproj_kernel(attn_ref, p_ref, prev_ref, next_ref, pw_ref, ps_ref, w_ref, x_ref, gt_ref, o_ref,
                    lhs_scr, pooled_scr, ext_scr, *, seq, tm):
    i = pl.program_id(0)
    heads = attn_ref.shape[0]
    att_w = heads * LANE
    n_pool_blocks = p_ref.shape[0]
    pool_group = pw_ref.shape[-1]

    @pl.when(pl.program_id(1) == 0)
    def _():
        for h in range(heads):
            lhs_scr[:, h * LANE:(h + 1) * LANE] = attn_ref[h]

        base = (i * tm) % seq
        has_prev = (base != 0).astype(F32)
        has_next = (base + tm != seq).astype(F32)
        pos = base + lax.broadcasted_iota(jnp.int32, (tm, LANE), 0)
        for cb in range(n_pool_blocks):
            w = POOL_WINDOWS[(cb * LANE) // pool_group]
            main = p_ref[cb]
            ext_scr[0:SUBLANE, :] = prev_ref[cb] * has_prev
            ext_scr[SUBLANE:SUBLANE + tm, :] = main
            ext_scr[SUBLANE + tm:2 * SUBLANE + tm, :] = next_ref[cb] * has_next
            total = ext_scr[pl.ds(SUBLANE - w // 2, tm), :]
            for off in range(-w // 2 + 1, w // 2):
                total = total + ext_scr[pl.ds(SUBLANE + off, tm), :]
            count = (jnp.minimum(pos + w // 2, seq) - jnp.maximum(pos - w // 2, 0)).astype(F32)
            pooled_scr[:, cb * LANE:(cb + 1) * LANE] = (total / count - main).astype(BF16)
        for gi in range(len(POOL_WINDOWS)):
            cols = slice(gi * pool_group, (gi + 1) * pool_group)
            mixed = jnp.dot(pooled_scr[:, cols], pw_ref[gi], preferred_element_type=F32)
            lhs_scr[:, att_w + gi * pool_group:att_w + (gi + 1) * pool_group] = (mixed * ps_ref[:, cols]).astype(BF16)

    half_w = o_ref.shape[1] // 2
    for half in range(2):
        cols = slice(half * half_w, (half + 1) * half_w)
        mix = jnp.dot(lhs_scr[...], w_ref[:, cols], preferred_element_type=F32)
        o_ref[:, cols] = x_ref[:, cols] + gt_ref[:, cols] * mix


def _outproj(attn_hm, proj_cm, pool_w_bf16, pool_scale, w_out_bf16, xt, gate, layer, seq, heads, tm, tn):
    t, d = xt.shape
    pool_w_total = pool_scale.shape[-1]
    npb = pool_w_total // LANE
    pool_blk = (3 * heads) // npb
    assert pool_blk * npb == 3 * heads and tm % SUBLANE == 0
    per_batch = seq // tm
    halo_per_tile = tm // SUBLANE
    n_halo = t // SUBLANE
    mix_w = w_out_bf16.shape[0]
    return pl.pallas_call(
        functools.partial(_outproj_kernel, seq=seq, tm=tm),
        grid=(t // tm, d // tn),
        in_specs=[pl.BlockSpec((heads, tm, LANE), lambda i, j: (0, i, 0)),
                  pl.BlockSpec((npb, tm, LANE), lambda i, j: (pool_blk, i, 0)),
                  pl.BlockSpec((npb, SUBLANE, LANE),
                               lambda i, j: (pool_blk, jnp.maximum(i * halo_per_tile - 1, 0), 0)),
                  pl.BlockSpec((npb, SUBLANE, LANE),
                               lambda i, j: (pool_blk, jnp.minimum((i + 1) * halo_per_tile, n_halo - 1), 0)),
                  pl.BlockSpec(pool_w_bf16.shape, lambda i, j: (0, 0, 0)),
                  pl.BlockSpec((None, 1, pool_w_total), lambda i, j: (layer, 0, 0)),
                  pl.BlockSpec((mix_w, tn), lambda i, j: (0, j)),
                  pl.BlockSpec((tm, tn), lambda i, j: (i, j)),
                  pl.BlockSpec((None, 1, tn), lambda i, j: (i // per_batch, 0, j))],
        out_specs=pl.BlockSpec((tm, tn), lambda i, j: (i, j)),
        out_shape=jax.ShapeDtypeStruct((t, d), F32),
        scratch_shapes=[pltpu.VMEM((tm, mix_w), BF16),
                        pltpu.VMEM((tm, pool_w_total), BF16),
                        pltpu.VMEM((tm + 2 * SUBLANE, LANE), F32)],
        compiler_params=_params("parallel", "arbitrary"),
        name="outproj",
    )(attn_hm, proj_cm, proj_cm, proj_cm, pool_w_bf16,
      pool_scale.reshape(pool_scale.shape[0], 1, pool_w_total), w_out_bf16, xt, gate)


def _peer_q_kernel(x_ref, g_ref, sc_ref, sh_ref, wq_ref, sk_ref, h_ref, s_ref):
    @pl.when(pl.program_id(1) == 0)
    def _():
        _modulated_rmsnorm_rows(x_ref, g_ref, sc_ref, sh_ref, h_ref)

    q = jnp.dot(h_ref[...], wq_ref[...], preferred_element_type=F32).astype(BF16)
    hps, _, nk, half = sk_ref.shape
    for hh in range(hps):
        for p in range(2):
            blk = hh * 2 + p
            s_ref[blk * nk:(blk + 1) * nk, :] = lax.dot_general(
                sk_ref[hh, p], q[:, blk * half:(blk + 1) * half], NT_DIMS, preferred_element_type=F32)


def _peer_q(xt, gain, scale, shift, wq_bf16, subkeys_bf16, layer, seq, tm):
    t, d = xt.shape
    ph, _, nk, half = subkeys_bf16.shape
    per_batch = seq // tm
    hps = 2 if ph % 2 == 0 else 1
    return pl.pallas_call(
        _peer_q_kernel,
        grid=(t // tm, ph // hps),
        in_specs=[pl.BlockSpec((tm, d), lambda i, h: (i, 0)),
                  pl.BlockSpec((None, 1, d), lambda i, h: (layer, 0, 0)),
                  pl.BlockSpec((None, 1, d), lambda i, h: (i // per_batch, 0, 0)),
                  pl.BlockSpec((None, 1, d), lambda i, h: (i // per_batch, 0, 0)),
                  pl.BlockSpec((d, hps * 2 * half), lambda i, h: (0, h)),
                  pl.BlockSpec((hps, 2, nk, half), lambda i, h: (h, 0, 0, 0))],
        out_specs=[pl.BlockSpec((tm, d), lambda i, h: (i, 0)),
                   pl.BlockSpec((hps * 2 * nk, tm), lambda i, h: (h, i))],
        out_shape=[jax.ShapeDtypeStruct((t, d), BF16),
                   jax.ShapeDtypeStruct((ph * 2 * nk, t), F32)],
        compiler_params=_params("parallel", "arbitrary"),
        name="peer_q",
    )(xt, gain.reshape(gain.shape[0], 1, d), scale, shift, wq_bf16, subkeys_bf16)


def _take_max(v, index, exact_ties):
    m = jnp.max(v, axis=0, keepdims=True)
    if exact_ties:
        first = jnp.min(jnp.where(v == m, index, float(PEER_TOPK * v.shape[0])), axis=0, keepdims=True)
        return m, index == first
    return m, v == m


def _ranked_top(v, index, exact_ties):
    rank = jnp.full(v.shape, float(PEER_TOPK), F32)
    tops = []
    for k in range(PEER_TOPK):
        m, hit = _take_max(v, index, exact_ties)
        v = jnp.where(hit, -jnp.inf, v)
        rank = jnp.where(hit, float(k), rank)
        tops.append(m)
    taken = jnp.sum(jnp.where(rank < float(PEER_TOPK), 1.0, 0.0), axis=0, keepdims=True)
    return rank, tops, taken


def _pair_counts(top1, top2, exact_ties):
    lanes = top1[0].shape[1]
    t1 = jnp.concatenate(top1, axis=0)
    a_iota = lax.broadcasted_iota(jnp.int32, (PEER_TOPK, lanes), 0).astype(F32)
    vals, idxs = [], []
    for b in range(PEER_TOPK):
        n_a = PEER_TOPK // (b + 1)
        rows = PEER_TOPK if n_a > SUBLANE else SUBLANE
        val = t1[:rows] + top2[b]
        if n_a < rows:
            val = jnp.where(a_iota[:rows] < float(n_a), val, -jnp.inf)
        vals.append(val)
        idxs.append(a_iota[:rows] * float(PEER_TOPK) + float(b))
    cand = jnp.concatenate(vals, axis=0)
    index = jnp.concatenate(idxs, axis=0)
    hits = jnp.zeros(cand.shape, F32)
    best = top1[0] + top2[0]
    z = jnp.zeros((1, lanes), F32)
    for _ in range(PEER_TOPK):
        m, hit = _take_max(cand, index, exact_ties)
        cand = jnp.where(hit, -jnp.inf, cand)
        hits = hits + jnp.where(hit, 1.0, 0.0)
        z = z + jnp.exp(m - best)
    cnt_lo = hits[0:SUBLANE]
    for b in range(1, PEER_TOPK):
        start = PEER_TOPK + (b - 1) * SUBLANE
        cnt_lo = cnt_lo + hits[start:start + SUBLANE]
    cnt = jnp.concatenate([cnt_lo, hits[SUBLANE:PEER_TOPK]], axis=0)
    return cnt, z, jnp.sum(cnt, axis=0, keepdims=True)


def _topk_strips(s_ref, c1_ref, p1_ref, r2_ref, p2_ref, exact_ties):
    nk = s_ref.shape[0] // 2
    index = lax.broadcasted_iota(jnp.int32, (nk, LANE), 0).astype(F32)
    tied = jnp.zeros((1, LANE), F32)
    for strip in range(s_ref.shape[1] // LANE):
        cols = slice(strip * LANE, (strip + 1) * LANE)
        s1 = s_ref[0:nk, cols]
        s2 = s_ref[nk:2 * nk, cols]
        rank1, top1, taken1 = _ranked_top(s1, index, exact_ties)
        rank2, top2, taken2 = _ranked_top(s2, index, exact_ties)
        cnt, z, taken = _pair_counts(top1, top2, exact_ties)
        c1 = jnp.zeros(s1.shape, F32)
        for a in range(PEER_TOPK):
            c1 = jnp.where(rank1 == float(a), cnt[a:a + 1], c1)
        c1_ref[:, cols] = c1
        p1_ref[:, cols] = jnp.exp(s1 - top1[0]) * (1.0 / z)
        r2_ref[:, cols] = rank2.astype(r2_ref.dtype)
        p2_ref[:, cols] = jnp.exp(s2 - top2[0]).astype(p2_ref.dtype)
        for n_taken in (taken1, taken2, taken):
            tied = jnp.maximum(tied, jnp.abs(n_taken - float(PEER_TOPK)))
    return tied


def _topk_kernel(s_ref, c1_ref, p1_ref, r2_ref, p2_ref):
    tied = _topk_strips(s_ref, c1_ref, p1_ref, r2_ref, p2_ref, exact_ties=False)

    @pl.when(jnp.max(tied) > 0.0)
    def _():
        _topk_strips(s_ref, c1_ref, p1_ref, r2_ref, p2_ref, exact_ties=True)


def _peer_topk(s_t, ph, nk, tk):
    t = s_t.shape[1]
    out_spec = pl.BlockSpec((nk, tk), lambda h, i: (h, i))
    return pl.pallas_call(
        _topk_kernel,
        grid=(ph, t // tk),
        in_specs=[pl.BlockSpec((2 * nk, tk), lambda h, i: (h, i))],
        out_specs=[out_spec] * 4,
        out_shape=[jax.ShapeDtypeStruct((ph * nk, t), dt) for dt in (F32, F32, BF16, BF16)],
        compiler_params=_params("parallel", "parallel"),
        name="peer_topk",
    )(s_t)


def _gelu_tanh(x):
    c = math.sqrt(2.0 / math.pi)
    z2 = x * (2.0 * c + (2.0 * c * 0.044715) * (x * x))
    return x / (1.0 + jnp.exp(-z2))


def _peer_ffn_kernel(h_ref, u_ref, vt_ref, c1_ref, p1_ref, r2_ref, p2_ref, o_ref, act_scr, bc_scr, *, nk):
    j = pl.program_id(1)
    te = u_ref.shape[0]
    tt = h_ref.shape[0]
    heads = c1_ref.shape[0]
    keys_per_step = te // nk
    steps_per_tile = SUBLANE // keys_per_step

    @pl.when(j == 0)
    def _():
        o_ref[...] = jnp.zeros_like(o_ref)

    rows = BF16_SUBLANES
    zero = jnp.zeros((rows, tt), BF16)
    keys_per_dot = max(keys_per_step // PEER_SCORE_DOTS, 1)
    for e in range(keys_per_step):
        if e % keys_per_dot == 0:
            a = lax.dot_general(u_ref[e * nk:(e + keys_per_dot) * nk, :], h_ref[...], NT_DIMS,
                                preferred_element_type=F32)
        row = (j % steps_per_tile) * keys_per_step + e
        for h in range(heads):
            bc_scr[e, h, 0] = jnp.broadcast_to(c1_ref[h, pl.ds(row, 1), :].astype(BF16), (rows, tt))
            bc_scr[e, h, 1] = jnp.broadcast_to(p1_ref[h, pl.ds(row, 1), :].astype(BF16), (rows, tt))
        for rc in range(nk // rows):
            gate = zero
            for h in range(heads):
                r2 = r2_ref[h * nk + rc * rows:h * nk + (rc + 1) * rows, :]
                p2 = p2_ref[h * nk + rc * rows:h * nk + (rc + 1) * rows, :]
                gate = gate + jnp.where(r2 < bc_scr[e, h, 0], bc_scr[e, h, 1] * p2, zero)
            dst = slice(e * nk + rc * rows, e * nk + (rc + 1) * rows)
            src = slice((e % keys_per_dot) * nk + rc * rows, (e % keys_per_dot) * nk + (rc + 1) * rows)
            act_scr[dst, :] = _gelu_tanh(a[src, :]).astype(BF16) * gate
    o_ref[...] += jnp.dot(vt_ref[...], act_scr[...], preferred_element_type=F32)


def _peer_ffn(h2, u_bf16, vt_blocks, tables, heads, nk, tt):
    t, d = h2.shape
    n_chunks, _, te = vt_blocks.shape
    keys_per_step = te // nk
    assert SUBLANE % keys_per_step == 0
    steps_per_tile = SUBLANE // keys_per_step
    c1, p1, r2, p2 = tables
    row_spec = pl.BlockSpec((heads, SUBLANE, tt), lambda i, j: (0, j // steps_per_tile, i))
    once = pl.Buffered(1)
    tab_spec = pl.BlockSpec((heads * nk, tt), lambda i, j: (0, i), pipeline_mode=once)
    return pl.pallas_call(
        functools.partial(_peer_ffn_kernel, nk=nk),
        grid=(t // tt, n_chunks),
        in_specs=[pl.BlockSpec((tt, d), lambda i, j: (i, 0), pipeline_mode=once),
                  pl.BlockSpec((te, d), lambda i, j: (j, 0)),
                  pl.BlockSpec((None, d, te), lambda i, j: (j, 0, 0)),
                  row_spec, row_spec, tab_spec, tab_spec],
        out_specs=pl.BlockSpec((d, tt), lambda i, j: (0, i), pipeline_mode=once),
        out_shape=jax.ShapeDtypeStruct((d, t), F32),
        scratch_shapes=[pltpu.VMEM((te, tt), BF16),
                        pltpu.VMEM((keys_per_step, heads, 2, BF16_SUBLANES, tt), BF16)],
        compiler_params=_params("parallel", "arbitrary"),
        name="peer_ffn",
    )(h2, u_bf16, vt_blocks, c1.reshape(heads, nk, t), p1.reshape(heads, nk, t), r2, p2)


def _residual_kernel(x_ref, pt_ref, gt_ref, g_ref, o_ref, *, final_norm):
    x = x_ref[...] + gt_ref[...] * jnp.transpose(pt_ref[...])
    if final_norm:
        x = (x * lax.rsqrt(jnp.mean(x * x, axis=-1, keepdims=True) + RMS_EPS)) * g_ref[...]
    o_ref[...] = x


def _residual(xt, peer_t, gate, g_final, seq, tm, final_norm):
    t, d = xt.shape
    per_batch = seq // tm
    return pl.pallas_call(
        functools.partial(_residual_kernel, final_norm=final_norm),
        grid=(t // tm,),
        in_specs=[pl.BlockSpec((tm, d), lambda i: (i, 0)),
                  pl.BlockSpec((d, tm), lambda i: (0, i)),
                  pl.BlockSpec((None, 1, d), lambda i: (i // per_batch, 0, 0)),
                  pl.BlockSpec((1, d), lambda i: (0, 0))],
        out_specs=pl.BlockSpec((tm, d), lambda i: (i, 0)),
        out_shape=jax.ShapeDtypeStruct((t, d), F32),
        compiler_params=_params("parallel"),
        name="residual",
    )(xt, peer_t, gate, g_final.reshape(1, d))


def kernel(x, c, w_ada, b_ada, g_mix, w_in, rel_bias, pool_w, pool_scale, w_out, g_ffn, peer_wq, peer_subkeys,
           peer_u, peer_v, g_final):
    batch, seq, d = x.shape
    t = batch * seq
    depth = w_ada.shape[0]
    pool_total = pool_scale.shape[-1]
    att_w = (w_in.shape[-1] - pool_total) // 3
    heads = att_w // HEAD_DIM
    ph, _, nk, _ = peer_subkeys.shape[1:]
    n_exp = peer_u.shape[1]
    tm = min(512, seq)
    tn = min(1024, d)
    te = 4 * nk

    xt = x.reshape(t, d)
    bias_tables = _attention_bias_tables(rel_bias)

    for layer in range(depth):
        mod = _ada(c, w_ada, b_ada, layer)
        sh1, sc1, gt1, sh2, sc2, gt2 = [m.reshape(batch, 1, d) for m in jnp.split(mod, 6, axis=-1)]

        proj_cm = _inproj(xt, g_mix, sc1, sh1, w_in[layer].astype(BF16), layer, seq, tm, tn)
        attn_hm = _attention(proj_cm, bias_tables, batch, seq, heads)
        xt = _outproj(attn_hm, proj_cm, pool_w[layer].astype(BF16), pool_scale, w_out[layer].astype(BF16),
                      xt, gt1, layer, seq, heads, tm, tn)

        h2, s_t = _peer_q(xt, g_ffn, sc2, sh2, peer_wq[layer].astype(BF16),
                          peer_subkeys[layer].astype(BF16), layer, seq, tm)
        tables = _peer_topk(s_t, ph, nk, tk=min(512, t))
        vt_blocks = jnp.transpose(peer_v[layer].reshape(n_exp // te, te, d), (0, 2, 1)).astype(BF16)
        peer_t = _peer_ffn(h2, peer_u[layer].astype(BF16), vt_blocks, tables, ph, nk, tt=min(1024, t))
        xt = _residual(xt, peer_t, gt2, g_final, seq, min(256, seq), final_norm=(layer == depth - 1))

    return xt.reshape(batch, seq, d)
```

```python
import functools
import math

import numpy as np
import jax
import jax.numpy as jnp
from jax import lax
from jax.experimental import pallas as pl
from jax.experimental.pallas import tpu as pltpu

HEAD_DIM = 128
DIL_PAIRS = ((128, 1), (512, 4), (2048, 16))
NUM_BUCKETS = 32
MAX_DISTANCE = 1024
POOL_WINDOWS = (2, 4, 8, 16)
PEER_TOPK = 16
RMS_EPS = 1e-6
NEG_INF = -1e30

LANE = 128
SUBLANE = 8
BF16_SUBLANES = 16
VMEM_LIMIT_BYTES = 58 * 1024 * 1024

ATT_SIDE = 64
ATT_TQ = 2 * ATT_SIDE
ATT_TK = ATT_TQ + 2 * ATT_SIDE
ATT_MACRO = ATT_TQ * max(d for _, d in DIL_PAIRS)
ATT_VARIANTS = 3
ATT_UNROLL = 16

F32 = jnp.float32
BF16 = jnp.bfloat16
NT_DIMS = (((1,), (1,)), ((), ()))


def _params(*semantics):
    return pltpu.CompilerParams(dimension_semantics=semantics, vmem_limit_bytes=VMEM_LIMIT_BYTES)


def _rows(start, size, stride):
    return pl.ds(start, size) if stride == 1 else pl.ds(start, size, stride=stride)


ADA_SPLIT = 4
PEER_SCORE_DOTS = 2


def _ada_kernel(ct_ref, *refs):
    w_refs, (b_ref, o_ref, acc_scr) = refs[:ADA_SPLIT], refs[ADA_SPLIT:]
    k = pl.program_id(0)
    batch = ct_ref.shape[1]
    tk, group = w_refs[0].shape

    @pl.when(k == 0)
    def _():
        acc_scr[...] = jnp.zeros_like(acc_scr)

    s = jax.nn.silu(ct_ref[...])
    s_cols = [jnp.broadcast_to(s[:, b:b + 1], (tk, LANE)) for b in range(batch)]
    for q, w_ref in enumerate(w_refs):
        for lt in range(group // LANE):
            w = w_ref[:, lt * LANE:(lt + 1) * LANE]
            cols = slice(q * group + lt * LANE, q * group + (lt + 1) * LANE)
            for b in range(batch):
                prod = w * s_cols[b]
                acc_scr[b, :, cols] += jnp.sum(prod.reshape(tk // SUBLANE, SUBLANE, LANE), axis=0)

    @pl.when(k == pl.num_programs(0) - 1)
    def _():
        o_ref[...] = jnp.zeros_like(o_ref)
        for b in range(batch):
            o_ref[b:b + 1, :] = jnp.sum(acc_scr[b], axis=0, keepdims=True) + b_ref[...]


def _ada(c, w_ada, b_ada, layer, tk=128):
    batch, d = c.shape
    n = w_ada.shape[-1]
    rows = -(-batch // SUBLANE) * SUBLANE
    return pl.pallas_call(
        _ada_kernel,
        grid=(d // tk,),
        in_specs=[pl.BlockSpec((tk, batch), lambda k: (k, 0))]
        + [pl.BlockSpec((None, tk, n // ADA_SPLIT), lambda k, q=q: (layer, k, q)) for q in range(ADA_SPLIT)]
        + [pl.BlockSpec((None, 1, n), lambda k: (layer, 0, 0))],
        out_specs=pl.BlockSpec((rows, n), lambda k: (0, 0)),
        out_shape=jax.ShapeDtypeStruct((rows, n), F32),
        scratch_shapes=[pltpu.VMEM((batch, SUBLANE, n), F32)],
        compiler_params=_params("arbitrary"),
        name="ada",
    )(jnp.transpose(c), *([w_ada] * ADA_SPLIT), b_ada.reshape(b_ada.shape[0], 1, n))[:batch]


NORM_ROWS = 16


def _modulated_rmsnorm_rows(x_ref, g_ref, sc_ref, sh_ref, o_ref):
    def body(c, carry):
        rows = pl.ds(pl.multiple_of(c * NORM_ROWS, NORM_ROWS), NORM_ROWS)
        x = x_ref[rows, :]
        y = x * lax.rsqrt(jnp.mean(x * x, axis=-1, keepdims=True) + RMS_EPS)
        o_ref[rows, :] = ((y * g_ref[...]) * (1.0 + sc_ref[...]) + sh_ref[...]).astype(o_ref.dtype)
        return carry

    lax.fori_loop(0, x_ref.shape[0] // NORM_ROWS, body, 0, unroll=4)


def _inproj_kernel(x_ref, g_ref, sc_ref, sh_ref, w_ref, o_ref, h_scr):
    @pl.when(pl.program_id(1) == 0)
    def _():
        _modulated_rmsnorm_rows(x_ref, g_ref, sc_ref, sh_ref, h_scr)

    blocks = o_ref.shape[0] // 2
    for half in range(2):
        acc = jnp.dot(h_scr[...], w_ref[:, half * blocks * LANE:(half + 1) * blocks * LANE],
                      preferred_element_type=F32)
        for k in range(blocks):
            o_ref[half * blocks + k] = acc[:, k * LANE:(k + 1) * LANE]


def _inproj(xt, gain, scale, shift, w_bf16, layer, seq, tm, tn):
    t, d = xt.shape
    n = w_bf16.shape[-1]
    per_batch = seq // tm
    return pl.pallas_call(
        _inproj_kernel,
        grid=(t // tm, n // tn),
        in_specs=[pl.BlockSpec((tm, d), lambda i, j: (i, 0)),
                  pl.BlockSpec((None, 1, d), lambda i, j: (layer, 0, 0)),
                  pl.BlockSpec((None, 1, d), lambda i, j: (i // per_batch, 0, 0)),
                  pl.BlockSpec((None, 1, d), lambda i, j: (i // per_batch, 0, 0)),
                  pl.BlockSpec((d, tn), lambda i, j: (0, j))],
        out_specs=pl.BlockSpec((tn // LANE, tm, LANE), lambda i, j: (j, i, 0)),
        out_shape=jax.ShapeDtypeStruct((n // LANE, t, LANE), F32),
        scratch_shapes=[pltpu.VMEM((tm, d), BF16)],
        compiler_params=_params("parallel", "arbitrary"),
        name="inproj",
    )(xt, gain.reshape(gain.shape[0], 1, d), scale, shift, w_bf16)


def _t5_bucket_np(rel):
    half = NUM_BUCKETS // 2
    n = -rel
    ret = np.where(n < 0, half, 0)
    n = np.abs(n)
    max_exact = half // 2
    nf = np.maximum(n, 1).astype(np.float32)
    large = max_exact + (np.log(nf / np.float32(max_exact)) / np.float32(math.log(MAX_DISTANCE / max_exact))
                         * np.float32(half - max_exact)).astype(np.int32)
    large = np.minimum(large, half - 1)
    return ret + np.where(n < max_exact, n, large)


def _attention_bucket_tables():
    i = np.arange(ATT_TQ)[:, None]
    j = np.arange(ATT_TK)[None, :]
    tables = []
    for _, dil in DIL_PAIRS:
        for off in (0, -ATT_SIDE, -2 * ATT_SIDE):
            rel = j + off - i
            tables.append(np.where(np.abs(rel) <= ATT_SIDE, _t5_bucket_np(rel * dil), -1))
    return np.stack(tables).astype(np.int32)


def _bias_kernel(rb_ref, bucket_ref, o_ref):
    h = pl.program_id(0)
    bucket = bucket_ref[...]
    out = jnp.full(bucket.shape, NEG_INF, F32)
    for b in range(NUM_BUCKETS):
        out = jnp.where(bucket == b, rb_ref[b, h], out)
    o_ref[...] = out


def _attention_bias_tables(rel_bias):
    buckets = jnp.asarray(_attention_bucket_tables())
    heads = rel_bias.shape[1]
    return pl.pallas_call(
        _bias_kernel,
        grid=(heads,),
        in_specs=[pl.BlockSpec(memory_space=pltpu.SMEM),
                  pl.BlockSpec(buckets.shape, lambda h: (0, 0, 0))],
        out_specs=pl.BlockSpec((None,) + buckets.shape, lambda h: (h, 0, 0, 0)),
        out_shape=jax.ShapeDtypeStruct((heads,) + buckets.shape, F32),
        compiler_params=_params("arbitrary"),
        name="attn_bias",
    )(rel_bias, buckets)


def _attn_kernel(q_ref, k_ref, v_ref, bias_ref, o_ref, ob_scr, lse_scr, *, seq):
    scale = HEAD_DIM ** -0.5

    def macro_tile(t, carry):
        p0 = pl.multiple_of(t * ATT_MACRO, ATT_MACRO)
        for g, (_, dil) in enumerate(DIL_PAIRS):
            length = seq // dil
            nblk = ATT_MACRO // (dil * ATT_TQ)

            def block(it, g=g, dil=dil, length=length):
                r = it % dil
                n = it // dil
                q0 = p0 // dil + n * ATT_TQ
                k0 = jnp.clip(q0 - ATT_SIDE, 0, length - ATT_TK)
                variant = jnp.where(q0 == 0, 0, jnp.where(q0 == length - ATT_TQ, 2, 1))
                q = q_ref[_rows(r + q0 * dil, ATT_TQ, dil), :].astype(BF16)
                k = k_ref[_rows(r + k0 * dil, ATT_TK, dil), :].astype(BF16)
                v = v_ref[_rows(r + k0 * dil, ATT_TK, dil), :].astype(BF16)
                s = lax.dot_general(q, k, NT_DIMS, preferred_element_type=F32)
                s = s * scale + bias_ref[g * ATT_VARIANTS + variant]
                m = jnp.max(s, axis=1, keepdims=True)
                p = jnp.exp(s - m)
                l = jnp.sum(p, axis=1, keepdims=True)
                o = jnp.dot(p.astype(BF16), v, preferred_element_type=F32) / l
                lse = m + jnp.log(l)
                dst = _rows(r + n * ATT_TQ * dil, ATT_TQ, dil)
                ob_scr[g, dst, :] = o
                lse_scr[g, dst, :] = jnp.broadcast_to(lse, (ATT_TQ, LANE))

            def block_group(gi, c, block=block):
                for u in range(ATT_UNROLL):
                    block(gi * ATT_UNROLL + u)
                return c

            lax.fori_loop(0, dil * nblk // ATT_UNROLL, block_group, 0)

        chunk = 256

        def merge(ci, c):
            rows = pl.ds(pl.multiple_of(ci * chunk, chunk), chunk)
            l0, l1, l2 = lse_scr[0, rows, :], lse_scr[1, rows, :], lse_scr[2, rows, :]
            mx = jnp.maximum(jnp.maximum(l0, l1), l2)
            e0, e1, e2 = jnp.exp(l0 - mx), jnp.exp(l1 - mx), jnp.exp(l2 - mx)
            den = e0 + e1 + e2
            out = (e0 / den) * ob_scr[0, rows, :] + (e1 / den) * ob_scr[1, rows, :] + (e2 / den) * ob_scr[2, rows, :]
            o_ref[pl.ds(p0 + pl.multiple_of(ci * chunk, chunk), chunk), :] = out.astype(o_ref.dtype)
            return c

        lax.fori_loop(0, ATT_MACRO // chunk, merge, 0)
        return carry

    lax.fori_loop(0, seq // ATT_MACRO, macro_tile, 0)


def _attention(proj_cm, bias_tables, batch, seq, heads):
    t = proj_cm.shape[1]
    assert seq % ATT_MACRO == 0 and seq // max(d for _, d in DIL_PAIRS) >= ATT_TK
    nb = len(DIL_PAIRS)

    def qkv_spec(off):
        return pl.BlockSpec((None, seq, LANE), lambda b, h, off=off: (off + h, b, 0))

    return pl.pallas_call(
        functools.partial(_attn_kernel, seq=seq),
        grid=(batch, heads),
        in_specs=[qkv_spec(0), qkv_spec(heads), qkv_spec(2 * heads),
                  pl.BlockSpec((None, nb * ATT_VARIANTS, ATT_TQ, ATT_TK), lambda b, h: (h, 0, 0, 0))],
        out_specs=pl.BlockSpec((None, seq, LANE), lambda b, h: (h, b, 0)),
        out_shape=jax.ShapeDtypeStruct((heads, t, LANE), BF16),
        scratch_shapes=[pltpu.VMEM((nb, ATT_MACRO, LANE), F32),
                        pltpu.VMEM((nb, ATT_MACRO, LANE), F32)],
        compiler_params=_params("parallel", "arbitrary"),
        name="attn",
    )(proj_cm, proj_cm, proj_cm, bias_tables)


def _outproj_kernel(attn_ref, p_ref, prev_ref, next_ref, pw_ref, ps_ref, w_ref, x_ref, gt_ref, o_ref,
                    lhs_scr, pooled_scr, ext_scr, *, seq, tm):
    i = pl.program_id(0)
    heads = attn_ref.shape[0]
    att_w = heads * LANE
    n_pool_blocks = p_ref.shape[0]
    pool_group = pw_ref.shape[-1]

    @pl.when(pl.program_id(1) == 0)
    def _():
        for h in range(heads):
            lhs_scr[:, h * LANE:(h + 1) * LANE] = attn_ref[h]

        base = (i * tm) % seq
        has_prev = (base != 0).astype(F32)
        has_next = (base + tm != seq).astype(F32)
        pos = base + lax.broadcasted_iota(jnp.int32, (tm, LANE), 0)
        for cb in range(n_pool_blocks):
            w = POOL_WINDOWS[(cb * LANE) // pool_group]
            main = p_ref[cb]
            ext_scr[0:SUBLANE, :] = prev_ref[cb] * has_prev
            ext_scr[SUBLANE:SUBLANE + tm, :] = main
            ext_scr[SUBLANE + tm:2 * SUBLANE + tm, :] = next_ref[cb] * has_next
            total = ext_scr[pl.ds(SUBLANE - w // 2, tm), :]
            for off in range(-w // 2 + 1, w // 2):
                total = total + ext_scr[pl.ds(SUBLANE + off, tm), :]
            count = (jnp.minimum(pos + w // 2, seq) - jnp.maximum(pos - w // 2, 0)).astype(F32)
            pooled_scr[:, cb * LANE:(cb + 1) * LANE] = (total / count - main).astype(BF16)
        for gi in range(len(POOL_WINDOWS)):
            cols = slice(gi * pool_group, (gi + 1) * pool_group)
            mixed = jnp.dot(pooled_scr[:, cols], pw_ref[gi], preferred_element_type=F32)
            lhs_scr[:, att_w + gi * pool_group:att_w + (gi + 1) * pool_group] = (mixed * ps_ref[:, cols]).astype(BF16)

    half_w = o_ref.shape[1] // 2
    for half in range(2):
        cols = slice(half * half_w, (half + 1) * half_w)
        mix = jnp.dot(lhs_scr[...], w_ref[:, cols], preferred_element_type=F32)
        o_ref[:, cols] = x_ref[:, cols] + gt_ref[:, cols] * mix


def _outproj(attn_hm, proj_cm, pool_w_bf16, pool_scale, w_out_bf16, xt, gate, layer, seq, heads, tm, tn):
    t, d = xt.shape
    pool_w_total = pool_scale.shape[-1]
    npb = pool_w_total // LANE
    pool_blk = (3 * heads) // npb
    assert pool_blk * npb == 3 * heads and tm % SUBLANE == 0
    per_batch = seq // tm
    halo_per_tile = tm // SUBLANE
    n_halo = t // SUBLANE
    mix_w = w_out_bf16.shape[0]
    return pl.pallas_call(
        functools.partial(_outproj_kernel, seq=seq, tm=tm),
        grid=(t // tm, d // tn),
        in_specs=[pl.BlockSpec((heads, tm, LANE), lambda i, j: (0, i, 0)),
                  pl.BlockSpec((npb, tm, LANE), lambda i, j: (pool_blk, i, 0)),
                  pl.BlockSpec((npb, SUBLANE, LANE),
                               lambda i, j: (pool_blk, jnp.maximum(i * halo_per_tile - 1, 0), 0)),
                  pl.BlockSpec((npb, SUBLANE, LANE),
                               lambda i, j: (pool_blk, jnp.minimum((i + 1) * halo_per_tile, n_halo - 1), 0)),
                  pl.BlockSpec(pool_w_bf16.shape, lambda i, j: (0, 0, 0)),
                  pl.BlockSpec((None, 1, pool_w_total), lambda i, j: (layer, 0, 0)),
                  pl.BlockSpec((mix_w, tn), lambda i, j: (0, j)),
                  pl.BlockSpec((tm, tn), lambda i, j: (i, j)),
                  pl.BlockSpec((None, 1, tn), lambda i, j: (i // per_batch, 0, j))],
        out_specs=pl.BlockSpec((tm, tn), lambda i, j: (i, j)),
        out_shape=jax.ShapeDtypeStruct((t, d), F32),
        scratch_shapes=[pltpu.VMEM((tm, mix_w), BF16),
                        pltpu.VMEM((tm, pool_w_total), BF16),
                        pltpu.VMEM((tm + 2 * SUBLANE, LANE), F32)],
        compiler_params=_params("parallel", "arbitrary"),
        name="outproj",
    )(attn_hm, proj_cm, proj_cm, proj_cm, pool_w_bf16,
      pool_scale.reshape(pool_scale.shape[0], 1, pool_w_total), w_out_bf16, xt, gate)


def _peer_q_kernel(x_ref, g_ref, sc_ref, sh_ref, wq_ref, sk_ref, h_ref, s_ref):
    @pl.when(pl.program_id(1) == 0)
    def _():
        _modulated_rmsnorm_rows(x_ref, g_ref, sc_ref, sh_ref, h_ref)

    hps, _, nk, half = sk_ref.shape
    pair = min(hps, 2)
    for h0 in range(0, hps, pair):
        cols = slice(h0 * 2 * half, (h0 + pair) * 2 * half)
        q = jnp.dot(h_ref[...], wq_ref[:, cols], preferred_element_type=F32).astype(BF16)
        for hh in range(pair):
            for p in range(2):
                blk = hh * 2 + p
                dst = (h0 * 2 + blk) * nk
                s_ref[dst:dst + nk, :] = lax.dot_general(
                    sk_ref[h0 + hh, p], q[:, blk * half:(blk + 1) * half], NT_DIMS, preferred_element_type=F32)


def _peer_q(xt, gain, scale, shift, wq_bf16, subkeys_bf16, layer, seq, tm):
    t, d = xt.shape
    ph, _, nk, half = subkeys_bf16.shape
    per_batch = seq // tm
    hps = next(n for n in (4, 2, 1) if ph % n == 0)
    return pl.pallas_call(
        _peer_q_kernel,
        grid=(t // tm, ph // hps),
        in_specs=[pl.BlockSpec((tm, d), lambda i, h: (i, 0)),
                  pl.BlockSpec((None, 1, d), lambda i, h: (layer, 0, 0)),
                  pl.BlockSpec((None, 1, d), lambda i, h: (i // per_batch, 0, 0)),
                  pl.BlockSpec((None, 1, d), lambda i, h: (i // per_batch, 0, 0)),
                  pl.BlockSpec((d, hps * 2 * half), lambda i, h: (0, h)),
                  pl.BlockSpec((hps, 2, nk, half), lambda i, h: (h, 0, 0, 0))],
        out_specs=[pl.BlockSpec((tm, d), lambda i, h: (i, 0)),
                   pl.BlockSpec((hps * 2 * nk, tm), lambda i, h: (h, i))],
        out_shape=[jax.ShapeDtypeStruct((t, d), BF16),
                   jax.ShapeDtypeStruct((ph * 2 * nk, t), F32)],
        compiler_params=_params("parallel", "arbitrary"),
        name="peer_q",
    )(xt, gain.reshape(gain.shape[0], 1, d), scale, shift, wq_bf16, subkeys_bf16)


def _take_max(v, index, exact_ties):
    m = jnp.max(v, axis=0, keepdims=True)
    if exact_ties:
        first = jnp.min(jnp.where(v == m, index, float(PEER_TOPK * v.shape[0])), axis=0, keepdims=True)
        return m, index == first
    return m, v == m


def _ranked_top(v, index, exact_ties, want_rank):
    rank = jnp.full(v.shape, float(PEER_TOPK), F32) if want_rank else None
    tops = []
    for k in range(PEER_TOPK):
        m, hit = _take_max(v, index, exact_ties)
        v = jnp.where(hit, -jnp.inf, v)
        if want_rank:
            rank = jnp.where(hit, float(k), rank)
        tops.append(m)
    taken = jnp.sum(jnp.where(v == -jnp.inf, 1.0, 0.0), axis=0, keepdims=True)
    return rank, tops, taken


def _pair_counts(top1, top2, exact_ties):
    lanes = top1[0].shape[1]
    t1 = jnp.concatenate(top1, axis=0)
    a_iota = lax.broadcasted_iota(jnp.int32, (PEER_TOPK, lanes), 0).astype(F32)
    vals, idxs, valid = [], [], []
    for b in range(PEER_TOPK):
        n_a = PEER_TOPK // (b + 1)
        rows = PEER_TOPK if n_a > SUBLANE else SUBLANE
        val = t1[:rows] + top2[b]
        if n_a < rows:
            val = jnp.where(a_iota[:rows] < float(n_a), val, -jnp.inf)
        vals.append(val)
        idxs.append(a_iota[:rows] * float(PEER_TOPK) + float(b))
        valid.append(jnp.where(a_iota[:rows] < float(n_a), 1.0, 0.0))
    cand = jnp.concatenate(vals, axis=0)
    index = jnp.concatenate(idxs, axis=0)
    best = top1[0] + top2[0]
    z = jnp.zeros((1, lanes), F32)
    for _ in range(PEER_TOPK):
        m, hit = _take_max(cand, index, exact_ties)
        cand = jnp.where(hit, -jnp.inf, cand)
        z = z + jnp.exp(m - best)
    hits = jnp.where(cand == -jnp.inf, jnp.concatenate(valid, axis=0), 0.0)
    cnt_lo = hits[0:SUBLANE]
    for b in range(1, PEER_TOPK):
        start = PEER_TOPK + (b - 1) * SUBLANE
        cnt_lo = cnt_lo + hits[start:start + SUBLANE]
    cnt = jnp.concatenate([cnt_lo, hits[SUBLANE:PEER_TOPK]], axis=0)
    return cnt, z, jnp.sum(cnt, axis=0, keepdims=True)


def _topk_strips(s_ref, c1_ref, p1_ref, r2_ref, p2_ref, exact_ties):
    nk = s_ref.shape[0] // 2
    index = lax.broadcasted_iota(jnp.int32, (nk, LANE), 0).astype(F32)
    tied = jnp.zeros((1, LANE), F32)
    for strip in range(s_ref.shape[1] // LANE):
        cols = slice(strip * LANE, (strip + 1) * LANE)
        s1 = s_ref[0:nk, cols]
        s2 = s_ref[nk:2 * nk, cols]
        rank1, top1, taken1 = _ranked_top(s1, index, exact_ties, want_rank=exact_ties)
        rank2, top2, taken2 = _ranked_top(s2, index, exact_ties, want_rank=True)
        cnt, z, taken = _pair_counts(top1, top2, exact_ties)
        c1 = jnp.zeros(s1.shape, F32)
        for a in range(PEER_TOPK):
            is_a = (rank1 == float(a)) if exact_ties else (s1 == top1[a])
            c1 = jnp.where(is_a, cnt[a:a + 1], c1)
        c1_ref[:, cols] = c1
        p1_ref[:, cols] = jnp.exp(s1 - top1[0]) * (1.0 / z)
        r2_ref[:, cols] = rank2.astype(r2_ref.dtype)
        p2_ref[:, cols] = jnp.exp(s2 - top2[0]).astype(p2_ref.dtype)
        for n_taken in (taken1, taken2, taken):
            tied = jnp.maximum(tied, jnp.abs(n_taken - float(PEER_TOPK)))
    return tied


def _topk_kernel(s_ref, c1_ref, p1_ref, r2_ref, p2_ref):
    tied = _topk_strips(s_ref, c1_ref, p1_ref, r2_ref, p2_ref, exact_ties=False)

    @pl.when(jnp.max(tied) > 0.0)
    def _():
        _topk_strips(s_ref, c1_ref, p1_ref, r2_ref, p2_ref, exact_ties=True)


def _peer_topk(s_t, ph, nk, tk):
    t = s_t.shape[1]
    out_spec = pl.BlockSpec((nk, tk), lambda h, i: (h, i))
    return pl.pallas_call(
        _topk_kernel,
        grid=(ph, t // tk),
        in_specs=[pl.BlockSpec((2 * nk, tk), lambda h, i: (h, i))],
        out_specs=[out_spec] * 4,
        out_shape=[jax.ShapeDtypeStruct((ph * nk, t), dt) for dt in (F32, F32, BF16, BF16)],
        compiler_params=_params("parallel", "parallel"),
        name="peer_topk",
    )(s_t)


def _gelu_tanh(x):
    c = math.sqrt(2.0 / math.pi)
    z2 = x * (2.0 * c + (2.0 * c * 0.044715) * (x * x))
    return x / (1.0 + jnp.exp(-z2))


def _peer_ffn_kernel(h_ref, u_ref, vt_ref, c1_ref, p1_ref, r2_ref, p2_ref, o_ref, act_scr, bc_scr, *, nk):
    j = pl.program_id(1)
    te = u_ref.shape[0]
    tt = h_ref.shape[0]
    heads = c1_ref.shape[0]
    keys_per_step = te // nk
    steps_per_tile = SUBLANE // keys_per_step

    @pl.when(j == 0)
    def _():
        o_ref[...] = jnp.zeros_like(o_ref)

    rows = BF16_SUBLANES
    zero = jnp.zeros((rows, tt), BF16)
    keys_per_dot = max(keys_per_step // PEER_SCORE_DOTS, 1)
    for e in range(keys_per_step):
        if e % keys_per_dot == 0:
            a = lax.dot_general(u_ref[e * nk:(e + keys_per_dot) * nk, :], h_ref[...], NT_DIMS,
                                preferred_element_type=F32)
        row = (j % steps_per_tile) * keys_per_step + e
        for h in range(heads):
            bc_scr[e, h, 0] = jnp.broadcast_to(c1_ref[h, pl.ds(row, 1), :].astype(BF16), (rows, tt))
            bc_scr[e, h, 1] = jnp.broadcast_to(p1_ref[h, pl.ds(row, 1), :].astype(BF16), (rows, tt))
        for rc in range(nk // rows):
            gate = zero
            for h in range(heads):
                r2 = r2_ref[h * nk + rc * rows:h * nk + (rc + 1) * rows, :]
                p2 = p2_ref[h * nk + rc * rows:h * nk + (rc + 1) * rows, :]
                gate = gate + jnp.where(r2 < bc_scr[e, h, 0], bc_scr[e, h, 1] * p2, zero)
            dst = slice(e * nk + rc * rows, e * nk + (rc + 1) * rows)
            src = slice((e % keys_per_dot) * nk + rc * rows, (e % keys_per_dot) * nk + (rc + 1) * rows)
            act_scr[dst, :] = _gelu_tanh(a[src, :]).astype(BF16) * gate
    o_ref[...] += jnp.dot(vt_ref[...], act_scr[...], preferred_element_type=F32)


def _peer_ffn(h2, u_bf16, vt_blocks, tables, heads, nk, tt):
    t, d = h2.shape
    n_chunks, _, te = vt_blocks.shape
    keys_per_step = te // nk
    assert SUBLANE % keys_per_step == 0
    steps_per_tile = SUBLANE // keys_per_step
    c1, p1, r2, p2 = tables
    row_spec = pl.BlockSpec((heads, SUBLANE, tt), lambda i, j: (0, j // steps_per_tile, i))
    once = pl.Buffered(1)
    tab_spec = pl.BlockSpec((heads * nk, tt), lambda i, j: (0, i), pipeline_mode=once)
    return pl.pallas_call(
        functools.partial(_peer_ffn_kernel, nk=nk),
        grid=(t // tt, n_chunks),
        in_specs=[pl.BlockSpec((tt, d), lambda i, j: (i, 0), pipeline_mode=once),
                  pl.BlockSpec((te, d), lambda i, j: (j, 0)),
                  pl.BlockSpec((None, d, te), lambda i, j: (j, 0, 0)),
                  row_spec, row_spec, tab_spec, tab_spec],
        out_specs=pl.BlockSpec((d, tt), lambda i, j: (0, i), pipeline_mode=once),
        out_shape=jax.ShapeDtypeStruct((d, t), F32),
        scratch_shapes=[pltpu.VMEM((te, tt), BF16),
                        pltpu.VMEM((keys_per_step, heads, 2, BF16_SUBLANES, tt), BF16)],
        compiler_params=_params("parallel", "arbitrary"),
        name="peer_ffn",
    )(h2, u_bf16, vt_blocks, c1.reshape(heads, nk, t), p1.reshape(heads, nk, t), r2, p2)


def _residual_kernel(x_ref, pt_ref, gt_ref, g_ref, o_ref, *, final_norm):
    x = x_ref[...] + gt_ref[...] * jnp.transpose(pt_ref[...])
    if final_norm:
        x = (x * lax.rsqrt(jnp.mean(x * x, axis=-1, keepdims=True) + RMS_EPS)) * g_ref[...]
    o_ref[...] = x


def _residual(xt, peer_t, gate, g_final, seq, tm, final_norm):
    t, d = xt.shape
    per_batch = seq // tm
    return pl.pallas_call(
        functools.partial(_residual_kernel, final_norm=final_norm),
        grid=(t // tm,),
        in_specs=[pl.BlockSpec((tm, d), lambda i: (i, 0)),
                  pl.BlockSpec((d, tm), lambda i: (0, i)),
                  pl.BlockSpec((None, 1, d), lambda i: (i // per_batch, 0, 0)),
                  pl.BlockSpec((1, d), lambda i: (0, 0))],
        out_specs=pl.BlockSpec((tm, d), lambda i: (i, 0)),
        out_shape=jax.ShapeDtypeStruct((t, d), F32),
        compiler_params=_params("parallel"),
        name="residual",
    )(xt, peer_t, gate, g_final.reshape(1, d))


def kernel(x, c, w_ada, b_ada, g_mix, w_in, rel_bias, pool_w, pool_scale, w_out, g_ffn, peer_wq, peer_subkeys,
           peer_u, peer_v, g_final):
    batch, seq, d = x.shape
    t = batch * seq
    depth = w_ada.shape[0]
    pool_total = pool_scale.shape[-1]
    att_w = (w_in.shape[-1] - pool_total) // 3
    heads = att_w // HEAD_DIM
    ph, _, nk, _ = peer_subkeys.shape[1:]
    n_exp = peer_u.shape[1]
    tm = min(512, seq)
    tn = min(1024, d)
    te = 8 * nk

    xt = x.reshape(t, d)
    bias_tables = _attention_bias_tables(rel_bias)

    for layer in range(depth):
        mod = _ada(c, w_ada, b_ada, layer)
        sh1, sc1, gt1, sh2, sc2, gt2 = [m.reshape(batch, 1, d) for m in jnp.split(mod, 6, axis=-1)]

        proj_cm = _inproj(xt, g_mix, sc1, sh1, w_in[layer].astype(BF16), layer, seq, tm, tn)
        attn_hm = _attention(proj_cm, bias_tables, batch, seq, heads)
        xt = _outproj(attn_hm, proj_cm, pool_w[layer].astype(BF16), pool_scale, w_out[layer].astype(BF16),
                      xt, gt1, layer, seq, heads, tm, tn)

        h2, s_t = _peer_q(xt, g_ffn, sc2, sh2, peer_wq[layer].astype(BF16),
                          peer_subkeys[layer].astype(BF16), layer, seq, tm)
        tables = _peer_topk(s_t, ph, nk, tk=min(512, t))
        vt_blocks = jnp.transpose(peer_v[layer].reshape(n_exp // te, te, d), (0, 2, 1)).astype(BF16)
        peer_t = _peer_ffn(h2, peer_u[layer].astype(BF16), vt_blocks, tables, ph, nk, tt=min(512, t))
        xt = _residual(xt, peer_t, gt2, g_final, seq, min(256, seq), final_norm=(layer == depth - 1))

    return xt.reshape(batch, seq, d)
```

```python
import functools
import math

import numpy as np
import jax
import jax.numpy as jnp
from jax import lax
from jax.experimental import pallas as pl
from jax.experimental.pallas import tpu as pltpu

HEAD_DIM = 128
DIL_PAIRS = ((128, 1), (512, 4), (2048, 16))
NUM_BUCKETS = 32
MAX_DISTANCE = 1024
POOL_WINDOWS = (2, 4, 8, 16)
PEER_TOPK = 16
RMS_EPS = 1e-6
NEG_INF = -1e30

LANE = 128
SUBLANE = 8
BF16_SUBLANES = 16
VMEM_LIMIT_BYTES = 58 * 1024 * 1024

ATT_SIDE = 64
ATT_TQ = 2 * ATT_SIDE
ATT_TK = ATT_TQ + 2 * ATT_SIDE
ATT_MACRO = ATT_TQ * max(d for _, d in DIL_PAIRS)
ATT_VARIANTS = 3

F32 = jnp.float32
BF16 = jnp.bfloat16
NT_DIMS = (((1,), (1,)), ((), ()))


def _params(*semantics):
    return pltpu.CompilerParams(dimension_semantics=semantics, vmem_limit_bytes=VMEM_LIMIT_BYTES)


def _rows(start, size, stride):
    return pl.ds(start, size) if stride == 1 else pl.ds(start, size, stride=stride)


ADA_SPLIT = 4
PEER_SCORE_DOTS = 2


def _ada_kernel(ct_ref, *refs):
    w_refs, (b_ref, o_ref, acc_scr) = refs[:ADA_SPLIT], refs[ADA_SPLIT:]
    k = pl.program_id(0)
    batch = ct_ref.shape[1]
    tk, group = w_refs[0].shape

    @pl.when(k == 0)
    def _():
        acc_scr[...] = jnp.zeros_like(acc_scr)

    s = jax.nn.silu(ct_ref[...])
    s_cols = [jnp.broadcast_to(s[:, b:b + 1], (tk, LANE)) for b in range(batch)]
    for q, w_ref in enumerate(w_refs):
        for lt in range(group // LANE):
            w = w_ref[:, lt * LANE:(lt + 1) * LANE]
            cols = slice(q * group + lt * LANE, q * group + (lt + 1) * LANE)
            for b in range(batch):
                prod = w * s_cols[b]
                acc_scr[b, :, cols] += jnp.sum(prod.reshape(tk // SUBLANE, SUBLANE, LANE), axis=0)

    @pl.when(k == pl.num_programs(0) - 1)
    def _():
        o_ref[...] = jnp.zeros_like(o_ref)
        for b in range(batch):
            o_ref[b:b + 1, :] = jnp.sum(acc_scr[b], axis=0, keepdims=True) + b_ref[...]


def _ada(c, w_ada, b_ada, layer, tk=128):
    batch, d = c.shape
    n = w_ada.shape[-1]
    rows = -(-batch // SUBLANE) * SUBLANE
    return pl.pallas_call(
        _ada_kernel,
        grid=(d // tk,),
        in_specs=[pl.BlockSpec((tk, batch), lambda k: (k, 0))]
        + [pl.BlockSpec((None, tk, n // ADA_SPLIT), lambda k, q=q: (layer, k, q)) for q in range(ADA_SPLIT)]
        + [pl.BlockSpec((None, 1, n), lambda k: (layer, 0, 0))],
        out_specs=pl.BlockSpec((rows, n), lambda k: (0, 0)),
        out_shape=jax.ShapeDtypeStruct((rows, n), F32),
        scratch_shapes=[pltpu.VMEM((batch, SUBLANE, n), F32)],
        compiler_params=_params("arbitrary"),
        name="ada",
    )(jnp.transpose(c), *([w_ada] * ADA_SPLIT), b_ada.reshape(b_ada.shape[0], 1, n))[:batch]


NORM_ROWS = 16


def _modulated_rmsnorm_rows(x_ref, g_ref, sc_ref, sh_ref, o_ref):
    def body(c, carry):
        rows = pl.ds(pl.multiple_of(c * NORM_ROWS, NORM_ROWS), NORM_ROWS)
        x = x_ref[rows, :]
        y = x * lax.rsqrt(jnp.mean(x * x, axis=-1, keepdims=True) + RMS_EPS)
        o_ref[rows, :] = ((y * g_ref[...]) * (1.0 + sc_ref[...]) + sh_ref[...]).astype(o_ref.dtype)
        return carry

    lax.fori_loop(0, x_ref.shape[0] // NORM_ROWS, body, 0, unroll=4)


def _inproj_kernel(x_ref, g_ref, sc_ref, sh_ref, w_ref, o_ref, h_scr):
    @pl.when(pl.program_id(1) == 0)
    def _():
        _modulated_rmsnorm_rows(x_ref, g_ref, sc_ref, sh_ref, h_scr)

    blocks = o_ref.shape[0] // 2
    for half in range(2):
        acc = jnp.dot(h_scr[...], w_ref[:, half * blocks * LANE:(half + 1) * blocks * LANE],
                      preferred_element_type=F32)
        for k in range(blocks):
            o_ref[half * blocks + k] = acc[:, k * LANE:(k + 1) * LANE].astype(o_ref.dtype)


def _inproj(xt, gain, scale, shift, w_bf16, layer, seq, tm, tn):
    t, d = xt.shape
    n = w_bf16.shape[-1]
    per_batch = seq // tm
    return pl.pallas_call(
        _inproj_kernel,
        grid=(t // tm, n // tn),
        in_specs=[pl.BlockSpec((tm, d), lambda i, j: (i, 0)),
                  pl.BlockSpec((None, 1, d), lambda i, j: (layer, 0, 0)),
                  pl.BlockSpec((None, 1, d), lambda i, j: (i // per_batch, 0, 0)),
                  pl.BlockSpec((None, 1, d), lambda i, j: (i // per_batch, 0, 0)),
                  pl.BlockSpec((d, tn), lambda i, j: (0, j))],
        out_specs=pl.BlockSpec((tn // LANE, tm, LANE), lambda i, j: (j, i, 0)),
        out_shape=jax.ShapeDtypeStruct((n // LANE, t, LANE), BF16),
        scratch_shapes=[pltpu.VMEM((tm, d), BF16)],
        compiler_params=_params("parallel", "arbitrary"),
        name="inproj",
    )(xt, gain.reshape(gain.shape[0], 1, d), scale, shift, w_bf16)


def _t5_bucket_np(rel):
    half = NUM_BUCKETS // 2
    n = -rel
    ret = np.where(n < 0, half, 0)
    n = np.abs(n)
    max_exact = half // 2
    nf = np.maximum(n, 1).astype(np.float32)
    large = max_exact + (np.log(nf / np.float32(max_exact)) / np.float32(math.log(MAX_DISTANCE / max_exact))
                         * np.float32(half - max_exact)).astype(np.int32)
    large = np.minimum(large, half - 1)
    return ret + np.where(n < max_exact, n, large)


def _attention_bucket_tables():
    i = np.arange(ATT_TQ)[:, None]
    j = np.arange(ATT_TK)[None, :]
    tables = []
    for _, dil in DIL_PAIRS:
        for off in (0, -ATT_SIDE, -2 * ATT_SIDE):
            rel = j + off - i
            tables.append(np.where(np.abs(rel) <= ATT_SIDE, _t5_bucket_np(rel * dil), -1))
    return np.stack(tables).astype(np.int32)


def _bias_kernel(rb_ref, bucket_ref, o_ref):
    h = pl.program_id(0)
    bucket = bucket_ref[...]
    out = jnp.full(bucket.shape, NEG_INF, F32)
    for b in range(NUM_BUCKETS):
        out = jnp.where(bucket == b, rb_ref[b, h], out)
    o_ref[...] = out


def _attention_bias_tables(rel_bias):
    buckets = jnp.asarray(_attention_bucket_tables())
    heads = rel_bias.shape[1]
    return pl.pallas_call(
        _bias_kernel,
        grid=(heads,),
        in_specs=[pl.BlockSpec(memory_space=pltpu.SMEM),
                  pl.BlockSpec(buckets.shape, lambda h: (0, 0, 0))],
        out_specs=pl.BlockSpec((None,) + buckets.shape, lambda h: (h, 0, 0, 0)),
        out_shape=jax.ShapeDtypeStruct((heads,) + buckets.shape, F32),
        compiler_params=_params("arbitrary"),
        name="attn_bias",
    )(rel_bias, buckets)


def _attn_kernel(*refs, seq):
    nb = len(DIL_PAIRS)
    qkv_refs = [refs[3 * g:3 * g + 3] for g in range(nb)]
    bias_ref, o_ref, ob_scr, lse_scr = refs[3 * nb:]
    scale = HEAD_DIM ** -0.5

    def macro_tile(t, carry):
        p0 = pl.multiple_of(t * ATT_MACRO, ATT_MACRO)
        for g, (_, dil) in enumerate(DIL_PAIRS):
            length = seq // dil
            nblk = ATT_MACRO // (dil * ATT_TQ)
            q_ref, k_ref, v_ref = qkv_refs[g]

            def block(it, g=g, dil=dil, length=length, q_ref=q_ref, k_ref=k_ref, v_ref=v_ref):
                r = it % dil
                n = it // dil
                lanes = slice(r * LANE, (r + 1) * LANE)
                q0 = p0 // dil + n * ATT_TQ
                k0 = jnp.clip(q0 - ATT_SIDE, 0, length - ATT_TK)
                variant = jnp.where(q0 == 0, 0, jnp.where(q0 == length - ATT_TQ, 2, 1))
                q = q_ref[pl.ds(pl.multiple_of(q0, ATT_TQ), ATT_TQ), lanes]
                k = k_ref[pl.ds(pl.multiple_of(k0, ATT_SIDE), ATT_TK), lanes]
                v = v_ref[pl.ds(pl.multiple_of(k0, ATT_SIDE), ATT_TK), lanes]
                s = lax.dot_general(q, k, NT_DIMS, preferred_element_type=F32)
                s = s * scale + bias_ref[g * ATT_VARIANTS + variant]
                m = jnp.max(s, axis=1, keepdims=True)
                p = jnp.exp(s - m)
                l = jnp.sum(p, axis=1, keepdims=True)
                o = jnp.dot(p.astype(BF16), v, preferred_element_type=F32) / l
                lse = m + jnp.log(l)
                dst = _rows(r + n * ATT_TQ * dil, ATT_TQ, dil)
                ob_scr[g, dst, :] = o
                lse_scr[g, dst, :] = jnp.broadcast_to(lse, (ATT_TQ, LANE))

            for it in range(dil * nblk):
                block(it)

        chunk = 256

        def merge(ci, c):
            rows = pl.ds(pl.multiple_of(ci * chunk, chunk), chunk)
            l0, l1, l2 = lse_scr[0, rows, :], lse_scr[1, rows, :], lse_scr[2, rows, :]
            mx = jnp.maximum(jnp.maximum(l0, l1), l2)
            e0, e1, e2 = jnp.exp(l0 - mx), jnp.exp(l1 - mx), jnp.exp(l2 - mx)
            den = e0 + e1 + e2
            out = (e0 / den) * ob_scr[0, rows, :] + (e1 / den) * ob_scr[1, rows, :] + (e2 / den) * ob_scr[2, rows, :]
            o_ref[pl.ds(p0 + pl.multiple_of(ci * chunk, chunk), chunk), :] = out.astype(o_ref.dtype)
            return c

        lax.fori_loop(0, ATT_MACRO // chunk, merge, 0)
        return carry

    lax.fori_loop(0, seq // ATT_MACRO, macro_tile, 0)


def _attention(proj_cm, bias_tables, batch, seq, heads):
    n_blocks, t, _ = proj_cm.shape
    assert seq % ATT_MACRO == 0 and seq // max(d for _, d in DIL_PAIRS) >= ATT_TK
    nb = len(DIL_PAIRS)
    views, specs = [], []
    for _, dil in DIL_PAIRS:
        view = proj_cm.reshape(n_blocks, t // dil, dil * LANE)
        for off in (0, heads, 2 * heads):
            views.append(view)
            specs.append(pl.BlockSpec((None, seq // dil, dil * LANE), lambda b, h, off=off: (off + h, b, 0)))

    return pl.pallas_call(
        functools.partial(_attn_kernel, seq=seq),
        grid=(batch, heads),
        in_specs=specs + [pl.BlockSpec((None, nb * ATT_VARIANTS, ATT_TQ, ATT_TK), lambda b, h: (h, 0, 0, 0))],
        out_specs=pl.BlockSpec((None, seq, LANE), lambda b, h: (h, b, 0)),
        out_shape=jax.ShapeDtypeStruct((heads, t, LANE), BF16),
        scratch_shapes=[pltpu.VMEM((nb, ATT_MACRO, LANE), F32),
                        pltpu.VMEM((nb, ATT_MACRO, LANE), F32)],
        compiler_params=_params("parallel", "arbitrary"),
        name="attn",
    )(*views, bias_tables)


def _outproj_kernel(attn_ref, p_ref, prev_ref, next_ref, pw_ref, ps_ref, w_ref, x_ref, gt_ref, o_ref,
                    lhs_scr, pooled_scr, ext_scr, *, seq, tm):
    i = pl.program_id(0)
    heads = attn_ref.shape[0]
    att_w = heads * LANE
    n_pool_blocks = p_ref.shape[0]
    pool_group = pw_ref.shape[-1]

    @pl.when(pl.program_id(1) == 0)
    def _():
        for h in range(heads):
            lhs_scr[:, h * LANE:(h + 1) * LANE] = attn_ref[h]

        base = (i * tm) % seq
        has_prev = (base != 0).astype(F32)
        has_next = (base + tm != seq).astype(F32)
        pos = base + lax.broadcasted_iota(jnp.int32, (tm, LANE), 0)
        for cb in range(n_pool_blocks):
            w = POOL_WINDOWS[(cb * LANE) // pool_group]
            main = p_ref[cb].astype(F32)
            ext_scr[0:SUBLANE, :] = prev_ref[cb].astype(F32)[SUBLANE:] * has_prev
            ext_scr[SUBLANE:SUBLANE + tm, :] = main
            ext_scr[SUBLANE + tm:2 * SUBLANE + tm, :] = next_ref[cb].astype(F32)[:SUBLANE] * has_next
            total = ext_scr[pl.ds(SUBLANE - w // 2, tm), :]
            for off in range(-w // 2 + 1, w // 2):
                total = total + ext_scr[pl.ds(SUBLANE + off, tm), :]
            count = (jnp.minimum(pos + w // 2, seq) - jnp.maximum(pos - w // 2, 0)).astype(F32)
            pooled_scr[:, cb * LANE:(cb + 1) * LANE] = (total / count - main).astype(BF16)
        for gi in range(len(POOL_WINDOWS)):
            cols = slice(gi * pool_group, (gi + 1) * pool_group)
            mixed = jnp.dot(pooled_scr[:, cols], pw_ref[gi], preferred_element_type=F32)
            lhs_scr[:, att_w + gi * pool_group:att_w + (gi + 1) * pool_group] = (mixed * ps_ref[:, cols]).astype(BF16)

    half_w = o_ref.shape[1] // 2
    for half in range(2):
        cols = slice(half * half_w, (half + 1) * half_w)
        mix = jnp.dot(lhs_scr[...], w_ref[:, cols], preferred_element_type=F32)
        o_ref[:, cols] = x_ref[:, cols] + gt_ref[:, cols] * mix


def _outproj(attn_hm, proj_cm, pool_w_bf16, pool_scale, w_out_bf16, xt, gate, layer, seq, heads, tm, tn):
    t, d = xt.shape
    pool_w_total = pool_scale.shape[-1]
    npb = pool_w_total // LANE
    pool_blk = (3 * heads) // npb
    halo = BF16_SUBLANES
    assert pool_blk * npb == 3 * heads and tm % halo == 0 and max(POOL_WINDOWS) // 2 <= SUBLANE
    per_batch = seq // tm
    halo_per_tile = tm // halo
    n_halo = t // halo
    mix_w = w_out_bf16.shape[0]
    return pl.pallas_call(
        functools.partial(_outproj_kernel, seq=seq, tm=tm),
        grid=(t // tm, d // tn),
        in_specs=[pl.BlockSpec((heads, tm, LANE), lambda i, j: (0, i, 0)),
                  pl.BlockSpec((npb, tm, LANE), lambda i, j: (pool_blk, i, 0)),
                  pl.BlockSpec((npb, halo, LANE),
                               lambda i, j: (pool_blk, jnp.maximum(i * halo_per_tile - 1, 0), 0)),
                  pl.BlockSpec((npb, halo, LANE),
                               lambda i, j: (pool_blk, jnp.minimum((i + 1) * halo_per_tile, n_halo - 1), 0)),
                  pl.BlockSpec(pool_w_bf16.shape, lambda i, j: (0, 0, 0)),
                  pl.BlockSpec((None, 1, pool_w_total), lambda i, j: (layer, 0, 0)),
                  pl.BlockSpec((mix_w, tn), lambda i, j: (0, j)),
                  pl.BlockSpec((tm, tn), lambda i, j: (i, j)),
                  pl.BlockSpec((None, 1, tn), lambda i, j: (i // per_batch, 0, j))],
        out_specs=pl.BlockSpec((tm, tn), lambda i, j: (i, j)),
        out_shape=jax.ShapeDtypeStruct((t, d), F32),
        scratch_shapes=[pltpu.VMEM((tm, mix_w), BF16),
                        pltpu.VMEM((tm, pool_w_total), BF16),
                        pltpu.VMEM((tm + 2 * SUBLANE, LANE), F32)],
        compiler_params=_params("parallel", "arbitrary"),
        name="outproj",
    )(attn_hm, proj_cm, proj_cm, proj_cm, pool_w_bf16,
      pool_scale.reshape(pool_scale.shape[0], 1, pool_w_total), w_out_bf16, xt, gate)


def _peer_q_kernel(x_ref, g_ref, sc_ref, sh_ref, wq_ref, sk_ref, h_ref, s_ref):
    @pl.when(pl.program_id(1) == 0)
    def _():
        _modulated_rmsnorm_rows(x_ref, g_ref, sc_ref, sh_ref, h_ref)

    hps, _, nk, half = sk_ref.shape
    pair = min(hps, 2)
    for h0 in range(0, hps, pair):
        cols = slice(h0 * 2 * half, (h0 + pair) * 2 * half)
        q = jnp.dot(h_ref[...], wq_ref[:, cols], preferred_element_type=F32).astype(BF16)
        for hh in range(pair):
            for p in range(2):
                blk = hh * 2 + p
                dst = (h0 * 2 + blk) * nk
                s_ref[dst:dst + nk, :] = lax.dot_general(
                    sk_ref[h0 + hh, p], q[:, blk * half:(blk + 1) * half], NT_DIMS, preferred_element_type=F32)


def _peer_q(xt, gain, scale, shift, wq_bf16, subkeys_bf16, layer, seq, tm):
    t, d = xt.shape
    ph, _, nk, half = subkeys_bf16.shape
    per_batch = seq // tm
    hps = next(n for n in (4, 2, 1) if ph % n == 0)
    return pl.pallas_call(
        _peer_q_kernel,
        grid=(t // tm, ph // hps),
        in_specs=[pl.BlockSpec((tm, d), lambda i, h: (i, 0)),
                  pl.BlockSpec((None, 1, d), lambda i, h: (layer, 0, 0)),
                  pl.BlockSpec((None, 1, d), lambda i, h: (i // per_batch, 0, 0)),
                  pl.BlockSpec((None, 1, d), lambda i, h: (i // per_batch, 0, 0)),
                  pl.BlockSpec((d, hps * 2 * half), lambda i, h: (0, h)),
                  pl.BlockSpec((hps, 2, nk, half), lambda i, h: (h, 0, 0, 0))],
        out_specs=[pl.BlockSpec((tm, d), lambda i, h: (i, 0)),
                   pl.BlockSpec((hps * 2 * nk, tm), lambda i, h: (h, i))],
        out_shape=[jax.ShapeDtypeStruct((t, d), BF16),
                   jax.ShapeDtypeStruct((ph * 2 * nk, t), F32)],
        compiler_params=_params("parallel", "arbitrary"),
        name="peer_q",
    )(xt, gain.reshape(gain.shape[0], 1, d), scale, shift, wq_bf16, subkeys_bf16)


def _take_max(v, index, exact_ties):
    m = jnp.max(v, axis=0, keepdims=True)
    if exact_ties:
        first = jnp.min(jnp.where(v == m, index, float(PEER_TOPK * v.shape[0])), axis=0, keepdims=True)
        return m, index == first
    return m, v == m


def _ranked_top(v, index, exact_ties, want_rank):
    rank = jnp.full(v.shape, float(PEER_TOPK), F32) if want_rank else None
    tops = []
    for k in range(PEER_TOPK):
        m, hit = _take_max(v, index, exact_ties)
        v = jnp.where(hit, -jnp.inf, v)
        if want_rank:
            rank = jnp.where(hit, float(k), rank)
        tops.append(m)
    taken = jnp.sum(jnp.where(v == -jnp.inf, 1.0, 0.0), axis=0, keepdims=True)
    return rank, tops, taken


def _pair_counts(top1, top2, exact_ties):
    lanes = top1[0].shape[1]
    t1 = jnp.concatenate(top1, axis=0)
    a_iota = lax.broadcasted_iota(jnp.int32, (PEER_TOPK, lanes), 0).astype(F32)
    vals, idxs, valid = [], [], []
    for b in range(PEER_TOPK):
        n_a = PEER_TOPK // (b + 1)
        rows = PEER_TOPK if n_a > SUBLANE else SUBLANE
        val = t1[:rows] + top2[b]
        if n_a < rows:
            val = jnp.where(a_iota[:rows] < float(n_a), val, -jnp.inf)
        vals.append(val)
        idxs.append(a_iota[:rows] * float(PEER_TOPK) + float(b))
        valid.append(jnp.where(a_iota[:rows] < float(n_a), 1.0, 0.0))
    cand = jnp.concatenate(vals, axis=0)
    index = jnp.concatenate(idxs, axis=0)
    best = top1[0] + top2[0]
    z = jnp.zeros((1, lanes), F32)
    for _ in range(PEER_TOPK):
        m, hit = _take_max(cand, index, exact_ties)
        cand = jnp.where(hit, -jnp.inf, cand)
        z = z + jnp.exp(m - best)
    hits = jnp.where(cand == -jnp.inf, jnp.concatenate(valid, axis=0), 0.0)
    cnt_lo = hits[0:SUBLANE]
    for b in range(1, PEER_TOPK):
        start = PEER_TOPK + (b - 1) * SUBLANE
        cnt_lo = cnt_lo + hits[start:start + SUBLANE]
    cnt = jnp.concatenate([cnt_lo, hits[SUBLANE:PEER_TOPK]], axis=0)
    return cnt, z, jnp.sum(cnt, axis=0, keepdims=True)


def _topk_strips(s_ref, c1_ref, p1_ref, r2_ref, p2_ref, exact_ties):
    nk = s_ref.shape[0] // 2
    index = lax.broadcasted_iota(jnp.int32, (nk, LANE), 0).astype(F32)
    tied = jnp.zeros((1, LANE), F32)
    for strip in range(s_ref.shape[1] // LANE):
        cols = slice(strip * LANE, (strip + 1) * LANE)
        s1 = s_ref[0:nk, cols]
        s2 = s_ref[nk:2 * nk, cols]
        rank1, top1, taken1 = _ranked_top(s1, index, exact_ties, want_rank=exact_ties)
        rank2, top2, taken2 = _ranked_top(s2, index, exact_ties, want_rank=True)
        cnt, z, taken = _pair_counts(top1, top2, exact_ties)
        c1 = jnp.zeros(s1.shape, F32)
        for a in range(PEER_TOPK):
            is_a = (rank1 == float(a)) if exact_ties else (s1 == top1[a])
            c1 = jnp.where(is_a, cnt[a:a + 1], c1)
        c1_ref[:, cols] = c1
        p1_ref[:, cols] = jnp.exp(s1 - top1[0]) * (1.0 / z)
        r2_ref[:, cols] = rank2.astype(r2_ref.dtype)
        p2_ref[:, cols] = jnp.exp(s2 - top2[0]).astype(p2_ref.dtype)
        for n_taken in (taken1, taken2, taken):
            tied = jnp.maximum(tied, jnp.abs(n_taken - float(PEER_TOPK)))
    return tied


def _topk_kernel(s_ref, c1_ref, p1_ref, r2_ref, p2_ref):
    tied = _topk_strips(s_ref, c1_ref, p1_ref, r2_ref, p2_ref, exact_ties=False)

    @pl.when(jnp.max(tied) > 0.0)
    def _():
        _topk_strips(s_ref, c1_ref, p1_ref, r2_ref, p2_ref, exact_ties=True)


def _peer_topk(s_t, ph, nk, tk):
    t = s_t.shape[1]
    out_spec = pl.BlockSpec((nk, tk), lambda h, i: (h, i))
    return pl.pallas_call(
        _topk_kernel,
        grid=(ph, t // tk),
        in_specs=[pl.BlockSpec((2 * nk, tk), lambda h, i: (h, i))],
        out_specs=[out_spec] * 4,
        out_shape=[jax.ShapeDtypeStruct((ph * nk, t), dt) for dt in (F32, F32, BF16, BF16)],
        compiler_params=_params("parallel", "parallel"),
        name="peer_topk",
    )(s_t)


def _gelu_tanh(x):
    c = math.sqrt(2.0 / math.pi)
    z2 = x * (2.0 * c + (2.0 * c * 0.044715) * (x * x))
    return x / (1.0 + jnp.exp(-z2))


def _peer_ffn_kernel(h_ref, u_ref, vt_ref, c1_ref, p1_ref, r2_ref, p2_ref, o_ref, act_scr, bc_scr, *, nk):
    j = pl.program_id(1)
    te = u_ref.shape[0]
    tt = h_ref.shape[0]
    heads = c1_ref.shape[0]
    keys_per_step = te // nk
    steps_per_tile = SUBLANE // keys_per_step

    @pl.when(j == 0)
    def _():
        o_ref[...] = jnp.zeros_like(o_ref)

    rows = BF16_SUBLANES
    zero = jnp.zeros((rows, tt), BF16)
    keys_per_dot = max(keys_per_step // PEER_SCORE_DOTS, 1)
    for e in range(keys_per_step):
        if e % keys_per_dot == 0:
            a = lax.dot_general(u_ref[e * nk:(e + keys_per_dot) * nk, :], h_ref[...], NT_DIMS,
                                preferred_element_type=F32)
        row = (j % steps_per_tile) * keys_per_step + e
        for h in range(heads):
            bc_scr[e, h, 0] = jnp.broadcast_to(c1_ref[h, pl.ds(row, 1), :].astype(BF16), (rows, tt))
            bc_scr[e, h, 1] = jnp.broadcast_to(p1_ref[h, pl.ds(row, 1), :].astype(BF16), (rows, tt))
        for rc in range(nk // rows):
            gate = zero
            for h in range(heads):
                r2 = r2_ref[h * nk + rc * rows:h * nk + (rc + 1) * rows, :]
                p2 = p2_ref[h * nk + rc * rows:h * nk + (rc + 1) * rows, :]
                gate = gate + jnp.where(r2 < bc_scr[e, h, 0], bc_scr[e, h, 1] * p2, zero)
            dst = slice(e * nk + rc * rows, e * nk + (rc + 1) * rows)
            src = slice((e % keys_per_dot) * nk + rc * rows, (e % keys_per_dot) * nk + (rc + 1) * rows)
            act_scr[dst, :] = _gelu_tanh(a[src, :]).astype(BF16) * gate
    o_ref[...] += jnp.dot(vt_ref[...], act_scr[...], preferred_element_type=F32)


def _peer_ffn(h2, u_bf16, vt_blocks, tables, heads, nk, tt):
    t, d = h2.shape
    n_chunks, _, te = vt_blocks.shape
    keys_per_step = te // nk
    assert SUBLANE % keys_per_step == 0
    steps_per_tile = SUBLANE // keys_per_step
    c1, p1, r2, p2 = tables
    row_spec = pl.BlockSpec((heads, SUBLANE, tt), lambda i, j: (0, j // steps_per_tile, i))
    once = pl.Buffered(1)
    tab_spec = pl.BlockSpec((heads * nk, tt), lambda i, j: (0, i), pipeline_mode=once)
    return pl.pallas_call(
        functools.partial(_peer_ffn_kernel, nk=nk),
        grid=(t // tt, n_chunks),
        in_specs=[pl.BlockSpec((tt, d), lambda i, j: (i, 0), pipeline_mode=once),
                  pl.BlockSpec((te, d), lambda i, j: (j, 0)),
                  pl.BlockSpec((None, d, te), lambda i, j: (j, 0, 0)),
                  row_spec, row_spec, tab_spec, tab_spec],
        out_specs=pl.BlockSpec((d, tt), lambda i, j: (0, i), pipeline_mode=once),
        out_shape=jax.ShapeDtypeStruct((d, t), F32),
        scratch_shapes=[pltpu.VMEM((te, tt), BF16),
                        pltpu.VMEM((keys_per_step, heads, 2, BF16_SUBLANES, tt), BF16)],
        compiler_params=_params("parallel", "arbitrary"),
        name="peer_ffn",
    )(h2, u_bf16, vt_blocks, c1.reshape(heads, nk, t), p1.reshape(heads, nk, t), r2, p2)


def _residual_kernel(x_ref, pt_ref, gt_ref, g_ref, o_ref, *, final_norm):
    x = x_ref[...] + gt_ref[...] * jnp.transpose(pt_ref[...])
    if final_norm:
        x = (x * lax.rsqrt(jnp.mean(x * x, axis=-1, keepdims=True) + RMS_EPS)) * g_ref[...]
    o_ref[...] = x


def _residual(xt, peer_t, gate, g_final, seq, tm, final_norm):
    t, d = xt.shape
    per_batch = seq // tm
    return pl.pallas_call(
        functools.partial(_residual_kernel, final_norm=final_norm),
        grid=(t // tm,),
        in_specs=[pl.BlockSpec((tm, d), lambda i: (i, 0)),
                  pl.BlockSpec((d, tm), lambda i: (0, i)),
                  pl.BlockSpec((None, 1, d), lambda i: (i // per_batch, 0, 0)),
                  pl.BlockSpec((1, d), lambda i: (0, 0))],
        out_specs=pl.BlockSpec((tm, d), lambda i: (i, 0)),
        out_shape=jax.ShapeDtypeStruct((t, d), F32),
        compiler_params=_params("parallel"),
        name="residual",
    )(xt, peer_t, gate, g_final.reshape(1, d))


def kernel(x, c, w_ada, b_ada, g_mix, w_in, rel_bias, pool_w, pool_scale, w_out, g_ffn, peer_wq, peer_subkeys,
           peer_u, peer_v, g_final):
    batch, seq, d = x.shape
    t = batch * seq
    depth = w_ada.shape[0]
    pool_total = pool_scale.shape[-1]
    att_w = (w_in.shape[-1] - pool_total) // 3
    heads = att_w // HEAD_DIM
    ph, _, nk, _ = peer_subkeys.shape[1:]
    n_exp = peer_u.shape[1]
    tm = min(512, seq)
    tn = min(1024, d)
    te = 8 * nk

    xt = x.reshape(t, d)
    bias_tables = _attention_bias_tables(rel_bias)

    for layer in range(depth):
        mod = _ada(c, w_ada, b_ada, layer)
        sh1, sc1, gt1, sh2, sc2, gt2 = [m.reshape(batch, 1, d) for m in jnp.split(mod, 6, axis=-1)]

        proj_cm = _inproj(xt, g_mix, sc1, sh1, w_in[layer].astype(BF16), layer, seq, tm, tn)
        attn_hm = _attention(proj_cm, bias_tables, batch, seq, heads)
        xt = _outproj(attn_hm, proj_cm, pool_w[layer].astype(BF16), pool_scale, w_out[layer].astype(BF16),
                      xt, gt1, layer, seq, heads, tm, tn)

        h2, s_t = _peer_q(xt, g_ffn, sc2, sh2, peer_wq[layer].astype(BF16),
                          peer_subkeys[layer].astype(BF16), layer, seq, tm)
        tables = _peer_topk(s_t, ph, nk, tk=min(512, t))
        vt_blocks = jnp.transpose(peer_v[layer].reshape(n_exp // te, te, d), (0, 2, 1)).astype(BF16)
        peer_t = _peer_ffn(h2, peer_u[layer].astype(BF16), vt_blocks, tables, ph, nk, tt=min(512, t))
        xt = _residual(xt, peer_t, gt2, g_final, seq, min(256, seq), final_norm=(layer == depth - 1))

    return xt.reshape(batch, seq, d)
```

```python
import functools
import math

import numpy as np
import jax
import jax.numpy as jnp
from jax import lax
from jax.experimental import pallas as pl
from jax.experimental.pallas import tpu as pltpu

HEAD_DIM = 128
DIL_PAIRS = ((128, 1), (512, 4), (2048, 16))
NUM_BUCKETS = 32
MAX_DISTANCE = 1024
POOL_WINDOWS = (2, 4, 8, 16)
PEER_TOPK = 16
RMS_EPS = 1e-6
NEG_INF = -1e30

LANE = 128
SUBLANE = 8
BF16_SUBLANES = 16
VMEM_LIMIT_BYTES = 58 * 1024 * 1024

ATT_SIDE = 64
ATT_TQ = 2 * ATT_SIDE
ATT_TK = ATT_TQ + 2 * ATT_SIDE
ATT_MACRO = ATT_TQ * max(d for _, d in DIL_PAIRS)
ATT_VARIANTS = 3
ATT_UNROLL = 16

F32 = jnp.float32
BF16 = jnp.bfloat16
NT_DIMS = (((1,), (1,)), ((), ()))


def _params(*semantics):
    return pltpu.CompilerParams(dimension_semantics=semantics, vmem_limit_bytes=VMEM_LIMIT_BYTES)


def _rows(start, size, stride):
    return pl.ds(start, size) if stride == 1 else pl.ds(start, size, stride=stride)


ADA_SPLIT = 4
PEER_SCORE_DOTS = 2


def _ada_kernel(ct_ref, *refs):
    w_refs, (b_ref, o_ref, acc_scr) = refs[:ADA_SPLIT], refs[ADA_SPLIT:]
    k = pl.program_id(0)
    batch = ct_ref.shape[1]
    tk, group = w_refs[0].shape

    @pl.when(k == 0)
    def _():
        acc_scr[...] = jnp.zeros_like(acc_scr)

    s = jax.nn.silu(ct_ref[...])
    s_cols = [jnp.broadcast_to(s[:, b:b + 1], (tk, LANE)) for b in range(batch)]
    for q, w_ref in enumerate(w_refs):
        for lt in range(group // LANE):
            w = w_ref[:, lt * LANE:(lt + 1) * LANE]
            cols = slice(q * group + lt * LANE, q * group + (lt + 1) * LANE)
            for b in range(batch):
                prod = w * s_cols[b]
                acc_scr[b, :, cols] += jnp.sum(prod.reshape(tk // SUBLANE, SUBLANE, LANE), axis=0)

    @pl.when(k == pl.num_programs(0) - 1)
    def _():
        o_ref[...] = jnp.zeros_like(o_ref)
        for b in range(batch):
            o_ref[b:b + 1, :] = jnp.sum(acc_scr[b], axis=0, keepdims=True) + b_ref[...]


def _ada(c, w_ada, b_ada, layer, tk=128):
    batch, d = c.shape
    n = w_ada.shape[-1]
    rows = -(-batch // SUBLANE) * SUBLANE
    return pl.pallas_call(
        _ada_kernel,
        grid=(d // tk,),
        in_specs=[pl.BlockSpec((tk, batch), lambda k: (k, 0))]
        + [pl.BlockSpec((None, tk, n // ADA_SPLIT), lambda k, q=q: (layer, k, q)) for q in range(ADA_SPLIT)]
        + [pl.BlockSpec((None, 1, n), lambda k: (layer, 0, 0))],
        out_specs=pl.BlockSpec((rows, n), lambda k: (0, 0)),
        out_shape=jax.ShapeDtypeStruct((rows, n), F32),
        scratch_shapes=[pltpu.VMEM((batch, SUBLANE, n), F32)],
        compiler_params=_params("arbitrary"),
        name="ada",
    )(jnp.transpose(c), *([w_ada] * ADA_SPLIT), b_ada.reshape(b_ada.shape[0], 1, n))[:batch]


NORM_ROWS = 16


def _modulated_rmsnorm_rows(x_ref, g_ref, sc_ref, sh_ref, o_ref):
    def body(c, carry):
        rows = pl.ds(pl.multiple_of(c * NORM_ROWS, NORM_ROWS), NORM_ROWS)
        x = x_ref[rows, :]
        y = x * lax.rsqrt(jnp.mean(x * x, axis=-1, keepdims=True) + RMS_EPS)
        o_ref[rows, :] = ((y * g_ref[...]) * (1.0 + sc_ref[...]) + sh_ref[...]).astype(o_ref.dtype)
        return carry

    lax.fori_loop(0, x_ref.shape[0] // NORM_ROWS, body, 0, unroll=4)


def _inproj_kernel(x_ref, g_ref, sc_ref, sh_ref, w_ref, o_ref, h_scr):
    @pl.when(pl.program_id(1) == 0)
    def _():
        _modulated_rmsnorm_rows(x_ref, g_ref, sc_ref, sh_ref, h_scr)

    blocks = o_ref.shape[0] // 2
    for half in range(2):
        acc = jnp.dot(h_scr[...], w_ref[:, half * blocks * LANE:(half + 1) * blocks * LANE],
                      preferred_element_type=F32)
        for k in range(blocks):
            o_ref[half * blocks + k] = acc[:, k * LANE:(k + 1) * LANE]


def _inproj(xt, gain, scale, shift, w_bf16, layer, seq, tm, tn):
    t, d = xt.shape
    n = w_bf16.shape[-1]
    per_batch = seq // tm
    return pl.pallas_call(
        _inproj_kernel,
        grid=(t // tm, n // tn),
        in_specs=[pl.BlockSpec((tm, d), lambda i, j: (i, 0)),
                  pl.BlockSpec((None, 1, d), lambda i, j: (layer, 0, 0)),
                  pl.BlockSpec((None, 1, d), lambda i, j: (i // per_batch, 0, 0)),
                  pl.BlockSpec((None, 1, d), lambda i, j: (i // per_batch, 0, 0)),
                  pl.BlockSpec((d, tn), lambda i, j: (0, j))],
        out_specs=pl.BlockSpec((tn // LANE, tm, LANE), lambda i, j: (j, i, 0)),
        out_shape=jax.ShapeDtypeStruct((n // LANE, t, LANE), F32),
        scratch_shapes=[pltpu.VMEM((tm, d), BF16)],
        compiler_params=_params("parallel", "arbitrary"),
        name="inproj",
    )(xt, gain.reshape(gain.shape[0], 1, d), scale, shift, w_bf16)


def _t5_bucket_np(rel):
    half = NUM_BUCKETS // 2
    n = -rel
    ret = np.where(n < 0, half, 0)
    n = np.abs(n)
    max_exact = half // 2
    nf = np.maximum(n, 1).astype(np.float32)
    large = max_exact + (np.log(nf / np.float32(max_exact)) / np.float32(math.log(MAX_DISTANCE / max_exact))
                         * np.float32(half - max_exact)).astype(np.int32)
    large = np.minimum(large, half - 1)
    return ret + np.where(n < max_exact, n, large)


def _attention_bucket_tables():
    i = np.arange(ATT_TQ)[:, None]
    j = np.arange(ATT_TK)[None, :]
    tables = []
    for _, dil in DIL_PAIRS:
        for off in (0, -ATT_SIDE, -2 * ATT_SIDE):
            rel = j + off - i
            tables.append(np.where(np.abs(rel) <= ATT_SIDE, _t5_bucket_np(rel * dil), -1))
    return np.stack(tables).astype(np.int32)


def _bias_kernel(rb_ref, bucket_ref, o_ref):
    h = pl.program_id(0)
    bucket = bucket_ref[...]
    out = jnp.full(bucket.shape, NEG_INF, F32)
    for b in range(NUM_BUCKETS):
        out = jnp.where(bucket == b, rb_ref[b, h], out)
    o_ref[...] = out


def _attention_bias_tables(rel_bias):
    buckets = jnp.asarray(_attention_bucket_tables())
    heads = rel_bias.shape[1]
    return pl.pallas_call(
        _bias_kernel,
        grid=(heads,),
        in_specs=[pl.BlockSpec(memory_space=pltpu.SMEM),
                  pl.BlockSpec(buckets.shape, lambda h: (0, 0, 0))],
        out_specs=pl.BlockSpec((None,) + buckets.shape, lambda h: (h, 0, 0, 0)),
        out_shape=jax.ShapeDtypeStruct((heads,) + buckets.shape, F32),
        compiler_params=_params("arbitrary"),
        name="attn_bias",
    )(rel_bias, buckets)


def _attn_kernel(q_ref, k_ref, v_ref, bias_ref, o_ref, ob_scr, lse_scr, *, seq):
    scale = HEAD_DIM ** -0.5

    def macro_tile(t, carry):
        p0 = pl.multiple_of(t * ATT_MACRO, ATT_MACRO)
        for g, (_, dil) in enumerate(DIL_PAIRS):
            length = seq // dil
            nblk = ATT_MACRO // (dil * ATT_TQ)

            def block(it, g=g, dil=dil, length=length):
                r = it % dil
                n = it // dil
                q0 = p0 // dil + n * ATT_TQ
                k0 = jnp.clip(q0 - ATT_SIDE, 0, length - ATT_TK)
                variant = jnp.where(q0 == 0, 0, jnp.where(q0 == length - ATT_TQ, 2, 1))
                q = q_ref[_rows(r + q0 * dil, ATT_TQ, dil), :].astype(BF16)
                k = k_ref[_rows(r + k0 * dil, ATT_TK, dil), :].astype(BF16)
                v = v_ref[_rows(r + k0 * dil, ATT_TK, dil), :].astype(BF16)
                s = lax.dot_general(q, k, NT_DIMS, preferred_element_type=F32)
                s = s * scale + bias_ref[g * ATT_VARIANTS + variant]
                m = jnp.max(s, axis=1, keepdims=True)
                p = jnp.exp(s - m)
                l = jnp.sum(p, axis=1, keepdims=True)
                o = jnp.dot(p.astype(BF16), v, preferred_element_type=F32) / l
                lse = m + jnp.log(l)
                dst = _rows(r + n * ATT_TQ * dil, ATT_TQ, dil)
                ob_scr[g, dst, :] = o
                lse_scr[g, dst, :] = jnp.broadcast_to(lse, (ATT_TQ, LANE))

            def block_group(gi, c, block=block):
                for u in range(ATT_UNROLL):
                    block(gi * ATT_UNROLL + u)
                return c

            lax.fori_loop(0, dil * nblk // ATT_UNROLL, block_group, 0)

        chunk = 256

        def merge(ci, c):
            rows = pl.ds(pl.multiple_of(ci * chunk, chunk), chunk)
            l0, l1, l2 = lse_scr[0, rows, :], lse_scr[1, rows, :], lse_scr[2, rows, :]
            mx = jnp.maximum(jnp.maximum(l0, l1), l2)
            e0, e1, e2 = jnp.exp(l0 - mx), jnp.exp(l1 - mx), jnp.exp(l2 - mx)
            den = e0 + e1 + e2
            out = (e0 / den) * ob_scr[0, rows, :] + (e1 / den) * ob_scr[1, rows, :] + (e2 / den) * ob_scr[2, rows, :]
            o_ref[pl.ds(p0 + pl.multiple_of(ci * chunk, chunk), chunk), :] = out.astype(o_ref.dtype)
            return c

        lax.fori_loop(0, ATT_MACRO // chunk, merge, 0)
        return carry

    lax.fori_loop(0, seq // ATT_MACRO, macro_tile, 0)


def _attention(proj_cm, bias_tables, batch, seq, heads):
    t = proj_cm.shape[1]
    assert seq % ATT_MACRO == 0 and seq // max(d for _, d in DIL_PAIRS) >= ATT_TK
    nb = len(DIL_PAIRS)

    def qkv_spec(off):
        return pl.BlockSpec((None, seq, LANE), lambda b, h, off=off: (off + h, b, 0))

    return pl.pallas_call(
        functools.partial(_attn_kernel, seq=seq),
        grid=(batch, heads),
        in_specs=[qkv_spec(0), qkv_spec(heads), qkv_spec(2 * heads),
                  pl.BlockSpec((None, nb * ATT_VARIANTS, ATT_TQ, ATT_TK), lambda b, h: (h, 0, 0, 0))],
        out_specs=pl.BlockSpec((None, seq, LANE), lambda b, h: (h, b, 0)),
        out_shape=jax.ShapeDtypeStruct((heads, t, LANE), BF16),
        scratch_shapes=[pltpu.VMEM((nb, ATT_MACRO, LANE), F32),
                        pltpu.VMEM((nb, ATT_MACRO, LANE), F32)],
        compiler_params=_params("parallel", "arbitrary"),
        name="attn",
    )(proj_cm, proj_cm, proj_cm, bias_tables)


def _outproj_kernel(attn_ref, p_ref, prev_ref, next_ref, pw_ref, ps_ref, w_ref, x_ref, gt_ref, o_ref,
                    lhs_scr, pooled_scr, ext_scr, *, seq, tm):
    i = pl.program_id(0)
    heads = attn_ref.shape[0]
    att_w = heads * LANE
    n_pool_blocks = p_ref.shape[0]
    pool_group = pw_ref.shape[-1]

    @pl.when(pl.program_id(1) == 0)
    def _():
        for h in range(heads):
            lhs_scr[:, h * LANE:(h + 1) * LANE] = attn_ref[h]

        base = (i * tm) % seq
        has_prev = (base != 0).astype(F32)
        has_next = (base + tm != seq).astype(F32)
        pos = base + lax.broadcasted_iota(jnp.int32, (tm, LANE), 0)
        for cb in range(n_pool_blocks):
            w = POOL_WINDOWS[(cb * LANE) // pool_group]
            main = p_ref[cb]
            ext_scr[0:SUBLANE, :] = prev_ref[cb] * has_prev
            ext_scr[SUBLANE:SUBLANE + tm, :] = main
            ext_scr[SUBLANE + tm:2 * SUBLANE + tm, :] = next_ref[cb] * has_next
            total = ext_scr[pl.ds(SUBLANE - w // 2, tm), :]
            for off in range(-w // 2 + 1, w // 2):
                total = total + ext_scr[pl.ds(SUBLANE + off, tm), :]
            count = (jnp.minimum(pos + w // 2, seq) - jnp.maximum(pos - w // 2, 0)).astype(F32)
            pooled_scr[:, cb * LANE:(cb + 1) * LANE] = (total / count - main).astype(BF16)
        for gi in range(len(POOL_WINDOWS)):
            cols = slice(gi * pool_group, (gi + 1) * pool_group)
            mixed = jnp.dot(pooled_scr[:, cols], pw_ref[gi], preferred_element_type=F32)
            lhs_scr[:, att_w + gi * pool_group:att_w + (gi + 1) * pool_group] = (mixed * ps_ref[:, cols]).astype(BF16)

    half_w = o_ref.shape[1] // 2
    for half in range(2):
        cols = slice(half * half_w, (half + 1) * half_w)
        mix = jnp.dot(lhs_scr[...], w_ref[:, cols], preferred_element_type=F32)
        o_ref[:, cols] = x_ref[:, cols] + gt_ref[:, cols] * mix


def _outproj(attn_hm, proj_cm, pool_w_bf16, pool_scale, w_out_bf16, xt, gate, layer, seq, heads, tm, tn):
    t, d = xt.shape
    pool_w_total = pool_scale.shape[-1]
    npb = pool_w_total // LANE
    pool_blk = (3 * heads) // npb
    assert pool_blk * npb == 3 * heads and tm % SUBLANE == 0
    per_batch = seq // tm
    halo_per_tile = tm // SUBLANE
    n_halo = t // SUBLANE
    mix_w = w_out_bf16.shape[0]
    return pl.pallas_call(
        functools.partial(_outproj_kernel, seq=seq, tm=tm),
        grid=(t // tm, d // tn),
        in_specs=[pl.BlockSpec((heads, tm, LANE), lambda i, j: (0, i, 0)),
                  pl.BlockSpec((npb, tm, LANE), lambda i, j: (pool_blk, i, 0)),
                  pl.BlockSpec((npb, SUBLANE, LANE),
                               lambda i, j: (pool_blk, jnp.maximum(i * halo_per_tile - 1, 0), 0)),
                  pl.BlockSpec((npb, SUBLANE, LANE),
                               lambda i, j: (pool_blk, jnp.minimum((i + 1) * halo_per_tile, n_halo - 1), 0)),
                  pl.BlockSpec(pool_w_bf16.shape, lambda i, j: (0, 0, 0)),
                  pl.BlockSpec((None, 1, pool_w_total), lambda i, j: (layer, 0, 0)),
                  pl.BlockSpec((mix_w, tn), lambda i, j: (0, j)),
                  pl.BlockSpec((tm, tn), lambda i, j: (i, j)),
                  pl.BlockSpec((None, 1, tn), lambda i, j: (i // per_batch, 0, j))],
        out_specs=pl.BlockSpec((tm, tn), lambda i, j: (i, j)),
        out_shape=jax.ShapeDtypeStruct((t, d), F32),
        scratch_shapes=[pltpu.VMEM((tm, mix_w), BF16),
                        pltpu.VMEM((tm, pool_w_total), BF16),
                        pltpu.VMEM((tm + 2 * SUBLANE, LANE), F32)],
        compiler_params=_params("parallel", "arbitrary"),
        name="outproj",
    )(attn_hm, proj_cm, proj_cm, proj_cm, pool_w_bf16,
      pool_scale.reshape(pool_scale.shape[0], 1, pool_w_total), w_out_bf16, xt, gate)


def _peer_q_kernel(x_ref, g_ref, sc_ref, sh_ref, wq_ref, sk_ref, h_ref, s_ref):
    @pl.when(pl.program_id(1) == 0)
    def _():
        _modulated_rmsnorm_rows(x_ref, g_ref, sc_ref, sh_ref, h_ref)

    hps, _, nk, half = sk_ref.shape
    pair = min(hps, 2)
    for h0 in range(0, hps, pair):
        cols = slice(h0 * 2 * half, (h0 + pair) * 2 * half)
        q = jnp.dot(h_ref[...], wq_ref[:, cols], preferred_element_type=F32).astype(BF16)
        for hh in range(pair):
            for p in range(2):
                blk = hh * 2 + p
                dst = (h0 * 2 + blk) * nk
                s_ref[dst:dst + nk, :] = lax.dot_general(
                    sk_ref[h0 + hh, p], q[:, blk * half:(blk + 1) * half], NT_DIMS, preferred_element_type=F32)


def _peer_q(xt, gain, scale, shift, wq_bf16, subkeys_bf16, layer, seq, tm):
    t, d = xt.shape
    ph, _, nk, half = subkeys_bf16.shape
    per_batch = seq // tm
    hps = next(n for n in (4, 2, 1) if ph % n == 0)
    return pl.pallas_call(
        _peer_q_kernel,
        grid=(t // tm, ph // hps),
        in_specs=[pl.BlockSpec((tm, d), lambda i, h: (i, 0)),
                  pl.BlockSpec((None, 1, d), lambda i, h: (layer, 0, 0)),
                  pl.BlockSpec((None, 1, d), lambda i, h: (i // per_batch, 0, 0)),
                  pl.BlockSpec((None, 1, d), lambda i, h: (i // per_batch, 0, 0)),
                  pl.BlockSpec((d, hps * 2 * half), lambda i, h: (0, h)),
                  pl.BlockSpec((hps, 2, nk, half), lambda i, h: (h, 0, 0, 0))],
        out_specs=[pl.BlockSpec((tm, d), lambda i, h: (i, 0)),
                   pl.BlockSpec((hps * 2 * nk, tm), lambda i, h: (h, i))],
        out_shape=[jax.ShapeDtypeStruct((t, d), BF16),
                   jax.ShapeDtypeStruct((ph * 2 * nk, t), F32)],
        compiler_params=_params("parallel", "arbitrary"),
        name="peer_q",
    )(xt, gain.reshape(gain.shape[0], 1, d), scale, shift, wq_bf16, subkeys_bf16)


def _take_max(v, index, exact_ties):
    m = jnp.max(v, axis=0, keepdims=True)
    if exact_ties:
        first = jnp.min(jnp.where(v == m, index, float(PEER_TOPK * v.shape[0])), axis=0, keepdims=True)
        return m, index == first
    return m, v == m


def _ranked_top(v, index, exact_ties, want_rank):
    rank = jnp.full(v.shape, float(PEER_TOPK), F32) if want_rank else None
    tops = []
    for k in range(PEER_TOPK):
        m, hit = _take_max(v, index, exact_ties)
        v = jnp.where(hit, -jnp.inf, v)
        if want_rank:
            rank = jnp.where(hit, float(k), rank)
        tops.append(m)
    taken = jnp.sum(jnp.where(v == -jnp.inf, 1.0, 0.0), axis=0, keepdims=True)
    return rank, tops, taken


def _pair_counts(top1, top2, exact_ties):
    lanes = top1[0].shape[1]
    t1 = jnp.concatenate(top1, axis=0)
    a_iota = lax.broadcasted_iota(jnp.int32, (PEER_TOPK, lanes), 0).astype(F32)
    vals, idxs, valid = [], [], []
    for b in range(PEER_TOPK):
        n_a = PEER_TOPK // (b + 1)
        rows = PEER_TOPK if n_a > SUBLANE else SUBLANE
        val = t1[:rows] + top2[b]
        if n_a < rows:
            val = jnp.where(a_iota[:rows] < float(n_a), val, -jnp.inf)
        vals.append(val)
        idxs.append(a_iota[:rows] * float(PEER_TOPK) + float(b))
        valid.append(jnp.where(a_iota[:rows] < float(n_a), 1.0, 0.0))
    cand = jnp.concatenate(vals, axis=0)
    index = jnp.concatenate(idxs, axis=0)
    best = top1[0] + top2[0]
    z = jnp.zeros((1, lanes), F32)
    for _ in range(PEER_TOPK):
        m, hit = _take_max(cand, index, exact_ties)
        cand = jnp.where(hit, -jnp.inf, cand)
        z = z + jnp.exp(m - best)
    hits = jnp.where(cand == -jnp.inf, jnp.concatenate(valid, axis=0), 0.0)
    cnt_lo = hits[0:SUBLANE]
    for b in range(1, PEER_TOPK):
        start = PEER_TOPK + (b - 1) * SUBLANE
        cnt_lo = cnt_lo + hits[start:start + SUBLANE]
    cnt = jnp.concatenate([cnt_lo, hits[SUBLANE:PEER_TOPK]], axis=0)
    return cnt, z, jnp.sum(cnt, axis=0, keepdims=True)


def _topk_strips(s_ref, c1_ref, p1_ref, r2_ref, p2_ref, exact_ties):
    nk = s_ref.shape[0] // 2
    index = lax.broadcasted_iota(jnp.int32, (nk, LANE), 0).astype(F32)
    tied = jnp.zeros((1, LANE), F32)
    for strip in range(s_ref.shape[1] // LANE):
        cols = slice(strip * LANE, (strip + 1) * LANE)
        s1 = s_ref[0:nk, cols]
        s2 = s_ref[nk:2 * nk, cols]
        rank1, top1, taken1 = _ranked_top(s1, index, exact_ties, want_rank=exact_ties)
        rank2, top2, taken2 = _ranked_top(s2, index, exact_ties, want_rank=True)
        cnt, z, taken = _pair_counts(top1, top2, exact_ties)
        c1 = jnp.zeros(s1.shape, F32)
        for a in range(PEER_TOPK):
            is_a = (rank1 == float(a)) if exact_ties else (s1 == top1[a])
            c1 = jnp.where(is_a, cnt[a:a + 1], c1)
        c1_ref[:, cols] = c1
        p1_ref[:, cols] = jnp.exp(s1 - top1[0]) * (1.0 / z)
        r2_ref[:, cols] = rank2.astype(r2_ref.dtype)
        p2_ref[:, cols] = jnp.exp(s2 - top2[0]).astype(p2_ref.dtype)
        for n_taken in (taken1, taken2, taken):
            tied = jnp.maximum(tied, jnp.abs(n_taken - float(PEER_TOPK)))
    return tied


def _topk_kernel(s_ref, c1_ref, p1_ref, r2_ref, p2_ref):
    tied = _topk_strips(s_ref, c1_ref, p1_ref, r2_ref, p2_ref, exact_ties=False)

    @pl.when(jnp.max(tied) > 0.0)
    def _():
        _topk_strips(s_ref, c1_ref, p1_ref, r2_ref, p2_ref, exact_ties=True)


def _peer_topk(s_t, ph, nk, tk):
    t = s_t.shape[1]
    out_spec = pl.BlockSpec((nk, tk), lambda h, i: (h, i))
    return pl.pallas_call(
        _topk_kernel,
        grid=(ph, t // tk),
        in_specs=[pl.BlockSpec((2 * nk, tk), lambda h, i: (h, i))],
        out_specs=[out_spec] * 4,
        out_shape=[jax.ShapeDtypeStruct((ph * nk, t), dt) for dt in (F32, F32, BF16, BF16)],
        compiler_params=_params("parallel", "parallel"),
        name="peer_topk",
    )(s_t)


def _gelu_tanh(x):
    c = math.sqrt(2.0 / math.pi)
    z2 = x * (2.0 * c + (2.0 * c * 0.044715) * (x * x))
    return x / (1.0 + jnp.exp(-z2))


def _peer_ffn_kernel(h_ref, u_ref, vt_ref, c1_ref, p1_ref, r2_ref, p2_ref, o_ref, act_scr, bc_scr, *, nk):
    j = pl.program_id(1)
    te = u_ref.shape[0]
    tt = h_ref.shape[0]
    heads = c1_ref.shape[0]
    keys_per_step = te // nk
    steps_per_tile = SUBLANE // keys_per_step

    @pl.when(j == 0)
    def _():
        o_ref[...] = jnp.zeros_like(o_ref)

    rows = BF16_SUBLANES
    zero = jnp.zeros((rows, tt), BF16)
    keys_per_dot = max(keys_per_step // PEER_SCORE_DOTS, 1)
    for e in range(keys_per_step):
        if e % keys_per_dot == 0:
            a = lax.dot_general(u_ref[e * nk:(e + keys_per_dot) * nk, :], h_ref[...], NT_DIMS,
                                preferred_element_type=F32)
        row = (j % steps_per_tile) * keys_per_step + e
        for h in range(heads):
            bc_scr[e, h, 0] = jnp.broadcast_to(c1_ref[h, pl.ds(row, 1), :].astype(BF16), (rows, tt))
            bc_scr[e, h, 1] = jnp.broadcast_to(p1_ref[h, pl.ds(row, 1), :].astype(BF16), (rows, tt))
        for rc in range(nk // rows):
            gate = zero
            for h in range(heads):
                r2 = r2_ref[h * nk + rc * rows:h * nk + (rc + 1) * rows, :]
                p2 = p2_ref[h * nk + rc * rows:h * nk + (rc + 1) * rows, :]
                gate = gate + jnp.where(r2 < bc_scr[e, h, 0], bc_scr[e, h, 1] * p2, zero)
            dst = slice(e * nk + rc * rows, e * nk + (rc + 1) * rows)
            src = slice((e % keys_per_dot) * nk + rc * rows, (e % keys_per_dot) * nk + (rc + 1) * rows)
            act_scr[dst, :] = _gelu_tanh(a[src, :]).astype(BF16) * gate
    o_ref[...] += jnp.dot(vt_ref[...], act_scr[...], preferred_element_type=F32)


def _peer_ffn(h2, u_bf16, vt_blocks, tables, heads, nk, tt):
    t, d = h2.shape
    n_chunks, _, te = vt_blocks.shape
    keys_per_step = te // nk
    assert SUBLANE % keys_per_step == 0
    steps_per_tile = SUBLANE // keys_per_step
    c1, p1, r2, p2 = tables
    row_spec = pl.BlockSpec((heads, SUBLANE, tt), lambda i, j: (0, j // steps_per_tile, i))
    once = pl.Buffered(1)
    tab_spec = pl.BlockSpec((heads * nk, tt), lambda i, j: (0, i), pipeline_mode=once)
    return pl.pallas_call(
        functools.partial(_peer_ffn_kernel, nk=nk),
        grid=(t // tt, n_chunks),
        in_specs=[pl.BlockSpec((tt, d), lambda i, j: (i, 0), pipeline_mode=once),
                  pl.BlockSpec((te, d), lambda i, j: (j, 0)),
                  pl.BlockSpec((None, d, te), lambda i, j: (j, 0, 0)),
                  row_spec, row_spec, tab_spec, tab_spec],
        out_specs=pl.BlockSpec((d, tt), lambda i, j: (0, i), pipeline_mode=once),
        out_shape=jax.ShapeDtypeStruct((d, t), F32),
        scratch_shapes=[pltpu.VMEM((te, tt), BF16),
                        pltpu.VMEM((keys_per_step, heads, 2, BF16_SUBLANES, tt), BF16)],
        compiler_params=_params("parallel", "arbitrary"),
        name="peer_ffn",
    )(h2, u_bf16, vt_blocks, c1.reshape(heads, nk, t), p1.reshape(heads, nk, t), r2, p2)


def _residual_kernel(x_ref, pt_ref, gt_ref, g_ref, o_ref, *, final_norm):
    x = x_ref[...] + gt_ref[...] * jnp.transpose(pt_ref[...])
    if final_norm:
        x = (x * lax.rsqrt(jnp.mean(x * x, axis=-1, keepdims=True) + RMS_EPS)) * g_ref[...]
    o_ref[...] = x


def _residual(xt, peer_t, gate, g_final, seq, tm, final_norm):
    t, d = xt.shape
    per_batch = seq // tm
    return pl.pallas_call(
        functools.partial(_residual_kernel, final_norm=final_norm),
        grid=(t // tm,),
        in_specs=[pl.BlockSpec((tm, d), lambda i: (i, 0)),
                  pl.BlockSpec((d, tm), lambda i: (0, i)),
                  pl.BlockSpec((None, 1, d), lambda i: (i // per_batch, 0, 0)),
                  pl.BlockSpec((1, d), lambda i: (0, 0))],
        out_specs=pl.BlockSpec((tm, d), lambda i: (i, 0)),
        out_shape=jax.ShapeDtypeStruct((t, d), F32),
        compiler_params=_params("parallel"),
        name="residual",
    )(xt, peer_t, gate, g_final.reshape(1, d))


def kernel(x, c, w_ada, b_ada, g_mix, w_in, rel_bias, pool_w, pool_scale, w_out, g_ffn, peer_wq, peer_subkeys,
           peer_u, peer_v, g_final):
    batch, seq, d = x.shape
    t = batch * seq
    depth = w_ada.shape[0]
    pool_total = pool_scale.shape[-1]
    att_w = (w_in.shape[-1] - pool_total) // 3
    heads = att_w // HEAD_DIM
    ph, _, nk, _ = peer_subkeys.shape[1:]
    n_exp = peer_u.shape[1]
    tm = min(512, seq)
    tn = min(1024, d)
    te = 8 * nk

    xt = x.reshape(t, d)
    bias_tables = _attention_bias_tables(rel_bias)

    for layer in range(depth):
        mod = _ada(c, w_ada, b_ada, layer)
        sh1, sc1, gt1, sh2, sc2, gt2 = [m.reshape(batch, 1, d) for m in jnp.split(mod, 6, axis=-1)]

        proj_cm = _inproj(xt, g_mix, sc1, sh1, w_in[layer].astype(BF16), layer, seq, tm, tn)
        attn_hm = _attention(proj_cm, bias_tables, batch, seq, heads)
        xt = _outproj(attn_hm, proj_cm, pool_w[layer].astype(BF16), pool_scale, w_out[layer].astype(BF16),
                      xt, gt1, layer, seq, heads, tm, tn)

        h2, s_t = _peer_q(xt, g_ffn, sc2, sh2, peer_wq[layer].astype(BF16),
                          peer_subkeys[layer].astype(BF16), layer, seq, tm)
        tables = _peer_topk(s_t, ph, nk, tk=min(1024, t))
        vt_blocks = jnp.transpose(peer_v[layer].reshape(n_exp // te, te, d), (0, 2, 1)).astype(BF16)
        peer_t = _peer_ffn(h2, peer_u[layer].astype(BF16), vt_blocks, tables, ph, nk, tt=min(512, t))
        xt = _residual(xt, peer_t, gt2, g_final, seq, min(256, seq), final_norm=(layer == depth - 1))

    return xt.reshape(batch, seq, d)
```

```python
import functools
import math

import numpy as np
import jax
import jax.numpy as jnp
from jax import lax
from jax.experimental import pallas as pl
from jax.experimental.pallas import tpu as pltpu

HEAD_DIM = 128
DIL_PAIRS = ((128, 1), (512, 4), (2048, 16))
NUM_BUCKETS = 32
MAX_DISTANCE = 1024
POOL_WINDOWS = (2, 4, 8, 16)
PEER_TOPK = 16
RMS_EPS = 1e-6
NEG_INF = -1e30

LANE = 128
SUBLANE = 8
BF16_SUBLANES = 16
VMEM_LIMIT_BYTES = 58 * 1024 * 1024

ATT_SIDE = 64
ATT_TQ = 2 * ATT_SIDE
ATT_TK = ATT_TQ + 2 * ATT_SIDE
ATT_MACRO = ATT_TQ * max(d for _, d in DIL_PAIRS)
ATT_VARIANTS = 3
ATT_UNROLL = 16

F32 = jnp.float32
BF16 = jnp.bfloat16
NT_DIMS = (((1,), (1,)), ((), ()))


def _params(*semantics):
    return pltpu.CompilerParams(dimension_semantics=semantics, vmem_limit_bytes=VMEM_LIMIT_BYTES)


def _rows(start, size, stride):
    return pl.ds(start, size) if stride == 1 else pl.ds(start, size, stride=stride)


ADA_SPLIT = 4
PEER_SCORE_DOTS = 2


def _ada_kernel(ct_ref, *refs):
    w_refs, (b_ref, o_ref, acc_scr) = refs[:ADA_SPLIT], refs[ADA_SPLIT:]
    k = pl.program_id(0)
    batch = ct_ref.shape[1]
    tk, group = w_refs[0].shape

    @pl.when(k == 0)
    def _():
        acc_scr[...] = jnp.zeros_like(acc_scr)

    s = jax.nn.silu(ct_ref[...])
    s_cols = [jnp.broadcast_to(s[:, b:b + 1], (tk, LANE)) for b in range(batch)]
    for q, w_ref in enumerate(w_refs):
        for lt in range(group // LANE):
            w = w_ref[:, lt * LANE:(lt + 1) * LANE]
            cols = slice(q * group + lt * LANE, q * group + (lt + 1) * LANE)
            for b in range(batch):
                prod = w * s_cols[b]
                acc_scr[b, :, cols] += jnp.sum(prod.reshape(tk // SUBLANE, SUBLANE, LANE), axis=0)

    @pl.when(k == pl.num_programs(0) - 1)
    def _():
        o_ref[...] = jnp.zeros_like(o_ref)
        for b in range(batch):
            o_ref[b:b + 1, :] = jnp.sum(acc_scr[b], axis=0, keepdims=True) + b_ref[...]


def _ada(c, w_ada, b_ada, layer, tk=128):
    batch, d = c.shape
    n = w_ada.shape[-1]
    rows = -(-batch // SUBLANE) * SUBLANE
    return pl.pallas_call(
        _ada_kernel,
        grid=(d // tk,),
        in_specs=[pl.BlockSpec((tk, batch), lambda k: (k, 0))]
        + [pl.BlockSpec((None, tk, n // ADA_SPLIT), lambda k, q=q: (layer, k, q)) for q in range(ADA_SPLIT)]
        + [pl.BlockSpec((None, 1, n), lambda k: (layer, 0, 0))],
        out_specs=pl.BlockSpec((rows, n), lambda k: (0, 0)),
        out_shape=jax.ShapeDtypeStruct((rows, n), F32),
        scratch_shapes=[pltpu.VMEM((batch, SUBLANE, n), F32)],
        compiler_params=_params("arbitrary"),
        name="ada",
    )(jnp.transpose(c), *([w_ada] * ADA_SPLIT), b_ada.reshape(b_ada.shape[0], 1, n))[:batch]


NORM_ROWS = 16


def _modulated_rmsnorm_rows(x_ref, g_ref, sc_ref, sh_ref, o_ref):
    def body(c, carry):
        rows = pl.ds(pl.multiple_of(c * NORM_ROWS, NORM_ROWS), NORM_ROWS)
        x = x_ref[rows, :]
        y = x * lax.rsqrt(jnp.mean(x * x, axis=-1, keepdims=True) + RMS_EPS)
        o_ref[rows, :] = ((y * g_ref[...]) * (1.0 + sc_ref[...]) + sh_ref[...]).astype(o_ref.dtype)
        return carry

    lax.fori_loop(0, x_ref.shape[0] // NORM_ROWS, body, 0, unroll=4)


def _inproj_kernel(x_ref, g_ref, sc_ref, sh_ref, w_ref, o_ref, h_scr):
    @pl.when(pl.program_id(1) == 0)
    def _():
        _modulated_rmsnorm_rows(x_ref, g_ref, sc_ref, sh_ref, h_scr)

    blocks = o_ref.shape[0] // 2
    for half in range(2):
        acc = jnp.dot(h_scr[...], w_ref[:, half * blocks * LANE:(half + 1) * blocks * LANE],
                      preferred_element_type=F32)
        for k in range(blocks):
            o_ref[half * blocks + k] = acc[:, k * LANE:(k + 1) * LANE]


def _inproj(xt, gain, scale, shift, w_bf16, layer, seq, tm, tn):
    t, d = xt.shape
    n = w_bf16.shape[-1]
    per_batch = seq // tm
    return pl.pallas_call(
        _inproj_kernel,
        grid=(t // tm, n // tn),
        in_specs=[pl.BlockSpec((tm, d), lambda i, j: (i, 0)),
                  pl.BlockSpec((None, 1, d), lambda i, j: (layer, 0, 0)),
                  pl.BlockSpec((None, 1, d), lambda i, j: (i // per_batch, 0, 0)),
                  pl.BlockSpec((None, 1, d), lambda i, j: (i // per_batch, 0, 0)),
                  pl.BlockSpec((d, tn), lambda i, j: (0, j))],
        out_specs=pl.BlockSpec((tn // LANE, tm, LANE), lambda i, j: (j, i, 0)),
        out_shape=jax.ShapeDtypeStruct((n // LANE, t, LANE), F32),
        scratch_shapes=[pltpu.VMEM((tm, d), BF16)],
        compiler_params=_params("parallel", "arbitrary"),
        name="inproj",
    )(xt, gain.reshape(gain.shape[0], 1, d), scale, shift, w_bf16)


def _t5_bucket_np(rel):
    half = NUM_BUCKETS // 2
    n = -rel
    ret = np.where(n < 0, half, 0)
    n = np.abs(n)
    max_exact = half // 2
    nf = np.maximum(n, 1).astype(np.float32)
    large = max_exact + (np.log(nf / np.float32(max_exact)) / np.float32(math.log(MAX_DISTANCE / max_exact))
                         * np.float32(half - max_exact)).astype(np.int32)
    large = np.minimum(large, half - 1)
    return ret + np.where(n < max_exact, n, large)


def _attention_bucket_tables():
    i = np.arange(ATT_TQ)[:, None]
    j = np.arange(ATT_TK)[None, :]
    tables = []
    for _, dil in DIL_PAIRS:
        for off in (0, -ATT_SIDE, -2 * ATT_SIDE):
            rel = j + off - i
            tables.append(np.where(np.abs(rel) <= ATT_SIDE, _t5_bucket_np(rel * dil), -1))
    return np.stack(tables).astype(np.int32)


def _bias_kernel(rb_ref, bucket_ref, o_ref):
    h = pl.program_id(0)
    bucket = bucket_ref[...]
    out = jnp.full(bucket.shape, NEG_INF, F32)
    for b in range(NUM_BUCKETS):
        out = jnp.where(bucket == b, rb_ref[b, h], out)
    o_ref[...] = out


def _attention_bias_tables(rel_bias):
    buckets = jnp.asarray(_attention_bucket_tables())
    heads = rel_bias.shape[1]
    return pl.pallas_call(
        _bias_kernel,
        grid=(heads,),
        in_specs=[pl.BlockSpec(memory_space=pltpu.SMEM),
                  pl.BlockSpec(buckets.shape, lambda h: (0, 0, 0))],
        out_specs=pl.BlockSpec((None,) + buckets.shape, lambda h: (h, 0, 0, 0)),
        out_shape=jax.ShapeDtypeStruct((heads,) + buckets.shape, F32),
        compiler_params=_params("arbitrary"),
        name="attn_bias",
    )(rel_bias, buckets)


def _attn_kernel(q_ref, k_ref, v_ref, bias_ref, o_ref, ob_scr, lse_scr, *, seq):
    scale = HEAD_DIM ** -0.5

    def macro_tile(t, carry):
        p0 = pl.multiple_of(t * ATT_MACRO, ATT_MACRO)
        for g, (_, dil) in enumerate(DIL_PAIRS):
            length = seq // dil
            nblk = ATT_MACRO // (dil * ATT_TQ)

            def block(it, g=g, dil=dil, length=length):
                r = it % dil
                n = it // dil
                q0 = p0 // dil + n * ATT_TQ
                k0 = jnp.clip(q0 - ATT_SIDE, 0, length - ATT_TK)
                variant = jnp.where(q0 == 0, 0, jnp.where(q0 == length - ATT_TQ, 2, 1))
                q = q_ref[_rows(r + q0 * dil, ATT_TQ, dil), :].astype(BF16)
                k = k_ref[_rows(r + k0 * dil, ATT_TK, dil), :].astype(BF16)
                v = v_ref[_rows(r + k0 * dil, ATT_TK, dil), :].astype(BF16)
                s = lax.dot_general(q, k, NT_DIMS, preferred_element_type=F32)
                s = s * scale + bias_ref[g * ATT_VARIANTS + variant]
                m = jnp.max(s, axis=1, keepdims=True)
                p = jnp.exp(s - m)
                l = jnp.sum(p, axis=1, keepdims=True)
                o = jnp.dot(p.astype(BF16), v, preferred_element_type=F32) / l
                lse = m + jnp.log(l)
                dst = _rows(r + n * ATT_TQ * dil, ATT_TQ, dil)
                ob_scr[g, dst, :] = o
                lse_scr[g, dst, :] = jnp.broadcast_to(lse, (ATT_TQ, LANE))

            def block_group(gi, c, block=block):
                for u in range(ATT_UNROLL):
                    block(gi * ATT_UNROLL + u)
                return c

            lax.fori_loop(0, dil * nblk // ATT_UNROLL, block_group, 0)

        chunk = 256

        def merge(ci, c):
            rows = pl.ds(pl.multiple_of(ci * chunk, chunk), chunk)
            l0, l1, l2 = lse_scr[0, rows, :], lse_scr[1, rows, :], lse_scr[2, rows, :]
            mx = jnp.maximum(jnp.maximum(l0, l1), l2)
            e0, e1, e2 = jnp.exp(l0 - mx), jnp.exp(l1 - mx), jnp.exp(l2 - mx)
            den = e0 + e1 + e2
            out = (e0 / den) * ob_scr[0, rows, :] + (e1 / den) * ob_scr[1, rows, :] + (e2 / den) * ob_scr[2, rows, :]
            o_ref[pl.ds(p0 + pl.multiple_of(ci * chunk, chunk), chunk), :] = out.astype(o_ref.dtype)
            return c

        lax.fori_loop(0, ATT_MACRO // chunk, merge, 0)
        return carry

    lax.fori_loop(0, seq // ATT_MACRO, macro_tile, 0)


def _attention(proj_cm, bias_tables, batch, seq, heads):
    t = proj_cm.shape[1]
    assert seq % ATT_MACRO == 0 and seq // max(d for _, d in DIL_PAIRS) >= ATT_TK
    nb = len(DIL_PAIRS)

    def qkv_spec(off):
        return pl.BlockSpec((None, seq, LANE), lambda b, h, off=off: (off + h, b, 0))

    return pl.pallas_call(
        functools.partial(_attn_kernel, seq=seq),
        grid=(batch, heads),
        in_specs=[qkv_spec(0), qkv_spec(heads), qkv_spec(2 * heads),
                  pl.BlockSpec((None, nb * ATT_VARIANTS, ATT_TQ, ATT_TK), lambda b, h: (h, 0, 0, 0))],
        out_specs=pl.BlockSpec((None, seq, LANE), lambda b, h: (h, b, 0)),
        out_shape=jax.ShapeDtypeStruct((heads, t, LANE), BF16),
        scratch_shapes=[pltpu.VMEM((nb, ATT_MACRO, LANE), F32),
                        pltpu.VMEM((nb, ATT_MACRO, LANE), F32)],
        compiler_params=_params("parallel", "arbitrary"),
        name="attn",
    )(proj_cm, proj_cm, proj_cm, bias_tables)


def _outproj_kernel(attn_ref, p_ref, prev_ref, next_ref, pw_ref, ps_ref, w_ref, x_ref, gt_ref, o_ref,
                    lhs_scr, pooled_scr, ext_scr, *, seq, tm):
    i = pl.program_id(0)
    heads = attn_ref.shape[0]
    att_w = heads * LANE
    n_pool_blocks = p_ref.shape[0]
    pool_group = pw_ref.shape[-1]

    @pl.when(pl.program_id(1) == 0)
    def _():
        for h in range(heads):
            lhs_scr[:, h * LANE:(h + 1) * LANE] = attn_ref[h]

        base = (i * tm) % seq
        has_prev = (base != 0).astype(F32)
        has_next = (base + tm != seq).astype(F32)
        pos = base + lax.broadcasted_iota(jnp.int32, (tm, LANE), 0)
        for cb in range(n_pool_blocks):
            w = POOL_WINDOWS[(cb * LANE) // pool_group]
            main = p_ref[cb]
            ext_scr[0:SUBLANE, :] = prev_ref[cb] * has_prev
            ext_scr[SUBLANE:SUBLANE + tm, :] = main
            ext_scr[SUBLANE + tm:2 * SUBLANE + tm, :] = next_ref[cb] * has_next
            total = ext_scr[pl.ds(SUBLANE - w // 2, tm), :]
            for off in range(-w // 2 + 1, w // 2):
                total = total + ext_scr[pl.ds(SUBLANE + off, tm), :]
            count = (jnp.minimum(pos + w // 2, seq) - jnp.maximum(pos - w // 2, 0)).astype(F32)
            pooled_scr[:, cb * LANE:(cb + 1) * LANE] = (total / count - main).astype(BF16)
        for gi in range(len(POOL_WINDOWS)):
            cols = slice(gi * pool_group, (gi + 1) * pool_group)
            mixed = jnp.dot(pooled_scr[:, cols], pw_ref[gi], preferred_element_type=F32)
            lhs_scr[:, att_w + gi * pool_group:att_w + (gi + 1) * pool_group] = (mixed * ps_ref[:, cols]).astype(BF16)

    half_w = o_ref.shape[1] // 2
    for half in range(2):
        cols = slice(half * half_w, (half + 1) * half_w)
        mix = jnp.dot(lhs_scr[...], w_ref[:, cols], preferred_element_type=F32)
        o_ref[:, cols] = x_ref[:, cols] + gt_ref[:, cols] * mix


def _outproj(attn_hm, proj_cm, pool_w_bf16, pool_scale, w_out_bf16, xt, gate, layer, seq, heads, tm, tn):
    t, d = xt.shape
    pool_w_total = pool_scale.shape[-1]
    npb = pool_w_total // LANE
    pool_blk = (3 * heads) // npb
    assert pool_blk * npb == 3 * heads and tm % SUBLANE == 0
    per_batch = seq // tm
    halo_per_tile = tm // SUBLANE
    n_halo = t // SUBLANE
    mix_w = w_out_bf16.shape[0]
    return pl.pallas_call(
        functools.partial(_outproj_kernel, seq=seq, tm=tm),
        grid=(t // tm, d // tn),
        in_specs=[pl.BlockSpec((heads, tm, LANE), lambda i, j: (0, i, 0)),
                  pl.BlockSpec((npb, tm, LANE), lambda i, j: (pool_blk, i, 0)),
                  pl.BlockSpec((npb, SUBLANE, LANE),
                               lambda i, j: (pool_blk, jnp.maximum(i * halo_per_tile - 1, 0), 0)),
                  pl.BlockSpec((npb, SUBLANE, LANE),
                               lambda i, j: (pool_blk, jnp.minimum((i + 1) * halo_per_tile, n_halo - 1), 0)),
                  pl.BlockSpec(pool_w_bf16.shape, lambda i, j: (0, 0, 0)),
                  pl.BlockSpec((None, 1, pool_w_total), lambda i, j: (layer, 0, 0)),
                  pl.BlockSpec((mix_w, tn), lambda i, j: (0, j)),
                  pl.BlockSpec((tm, tn), lambda i, j: (i, j)),
                  pl.BlockSpec((None, 1, tn), lambda i, j: (i // per_batch, 0, j))],
        out_specs=pl.BlockSpec((tm, tn), lambda i, j: (i, j)),
        out_shape=jax.ShapeDtypeStruct((t, d), F32),
        scratch_shapes=[pltpu.VMEM((tm, mix_w), BF16),
                        pltpu.VMEM((tm, pool_w_total), BF16),
                        pltpu.VMEM((tm + 2 * SUBLANE, LANE), F32)],
        compiler_params=_params("parallel", "arbitrary"),
        name="outproj",
    )(attn_hm, proj_cm, proj_cm, proj_cm, pool_w_bf16,
      pool_scale.reshape(pool_scale.shape[0], 1, pool_w_total), w_out_bf16, xt, gate)


def _peer_q_kernel(x_ref, g_ref, sc_ref, sh_ref, wq_ref, sk_ref, h_ref, s_ref):
    @pl.when(pl.program_id(1) == 0)
    def _():
        _modulated_rmsnorm_rows(x_ref, g_ref, sc_ref, sh_ref, h_ref)

    hps, _, nk, half = sk_ref.shape
    pair = min(hps, 2)
    for h0 in range(0, hps, pair):
        cols = slice(h0 * 2 * half, (h0 + pair) * 2 * half)
        q = jnp.dot(h_ref[...], wq_ref[:, cols], preferred_element_type=F32).astype(BF16)
        for hh in range(pair):
            for p in range(2):
                blk = hh * 2 + p
                dst = (h0 * 2 + blk) * nk
                s_ref[dst:dst + nk, :] = lax.dot_general(
                    sk_ref[h0 + hh, p], q[:, blk * half:(blk + 1) * half], NT_DIMS, preferred_element_type=F32)


def _peer_q(xt, gain, scale, shift, wq_bf16, subkeys_bf16, layer, seq, tm):
    t, d = xt.shape
    ph, _, nk, half = subkeys_bf16.shape
    per_batch = seq // tm
    hps = next(n for n in (4, 2, 1) if ph % n == 0)
    return pl.pallas_call(
        _peer_q_kernel,
        grid=(t // tm, ph // hps),
        in_specs=[pl.BlockSpec((tm, d), lambda i, h: (i, 0)),
                  pl.BlockSpec((None, 1, d), lambda i, h: (layer, 0, 0)),
                  pl.BlockSpec((None, 1, d), lambda i, h: (i // per_batch, 0, 0)),
                  pl.BlockSpec((None, 1, d), lambda i, h: (i // per_batch, 0, 0)),
                  pl.BlockSpec((d, hps * 2 * half), lambda i, h: (0, h)),
                  pl.BlockSpec((hps, 2, nk, half), lambda i, h: (h, 0, 0, 0))],
        out_specs=[pl.BlockSpec((tm, d), lambda i, h: (i, 0)),
                   pl.BlockSpec((hps * 2 * nk, tm), lambda i, h: (h, i))],
        out_shape=[jax.ShapeDtypeStruct((t, d), BF16),
                   jax.ShapeDtypeStruct((ph * 2 * nk, t), F32)],
        compiler_params=_params("parallel", "arbitrary"),
        name="peer_q",
    )(xt, gain.reshape(gain.shape[0], 1, d), scale, shift, wq_bf16, subkeys_bf16)


def _take_max(v, index, exact_ties):
    m = jnp.max(v, axis=0, keepdims=True)
    if exact_ties:
        first = jnp.min(jnp.where(v == m, index, float(PEER_TOPK * v.shape[0])), axis=0, keepdims=True)
        return m, index == first
    return m, v == m


def _ranked_top(v, index, exact_ties, want_rank):
    rank = jnp.full(v.shape, float(PEER_TOPK), F32) if want_rank else None
    tops = []
    for k in range(PEER_TOPK):
        m, hit = _take_max(v, index, exact_ties)
        v = jnp.where(hit, -jnp.inf, v)
        if want_rank:
            rank = jnp.where(hit, float(k), rank)
        tops.append(m)
    taken = jnp.sum(jnp.where(v == -jnp.inf, 1.0, 0.0), axis=0, keepdims=True)
    return rank, tops, taken


def _pair_counts(top1, top2, exact_ties):
    lanes = top1[0].shape[1]
    t1 = jnp.concatenate(top1, axis=0)
    a_iota = lax.broadcasted_iota(jnp.int32, (PEER_TOPK, lanes), 0).astype(F32)
    vals, idxs, valid = [], [], []
    for b in range(PEER_TOPK):
        n_a = PEER_TOPK // (b + 1)
        rows = PEER_TOPK if n_a > SUBLANE else SUBLANE
        val = t1[:rows] + top2[b]
        if n_a < rows:
            val = jnp.where(a_iota[:rows] < float(n_a), val, -jnp.inf)
        vals.append(val)
        idxs.append(a_iota[:rows] * float(PEER_TOPK) + float(b))
        valid.append(jnp.where(a_iota[:rows] < float(n_a), 1.0, 0.0))
    cand = jnp.concatenate(vals, axis=0)
    index = jnp.concatenate(idxs, axis=0)
    best = top1[0] + top2[0]
    z = jnp.zeros((1, lanes), F32)
    for _ in range(PEER_TOPK):
        m, hit = _take_max(cand, index, exact_ties)
        cand = jnp.where(hit, -jnp.inf, cand)
        z = z + jnp.exp(m - best)
    hits = jnp.where(cand == -jnp.inf, jnp.concatenate(valid, axis=0), 0.0)
    cnt_lo = hits[0:SUBLANE]
    for b in range(1, PEER_TOPK):
        start = PEER_TOPK + (b - 1) * SUBLANE
        cnt_lo = cnt_lo + hits[start:start + SUBLANE]
    cnt = jnp.concatenate([cnt_lo, hits[SUBLANE:PEER_TOPK]], axis=0)
    return cnt, z, jnp.sum(cnt, axis=0, keepdims=True)


def _topk_strips(s_ref, c1_ref, p1_ref, r2_ref, p2_ref, exact_ties, strips):
    nk = s_ref.shape[0] // 2
    index = lax.broadcasted_iota(jnp.int32, (nk, LANE), 0).astype(F32)
    tied_per_strip = []
    for strip in strips:
        tied = jnp.zeros((1, LANE), F32)
        cols = slice(strip * LANE, (strip + 1) * LANE)
        s1 = s_ref[0:nk, cols]
        s2 = s_ref[nk:2 * nk, cols]
        rank1, top1, taken1 = _ranked_top(s1, index, exact_ties, want_rank=exact_ties)
        rank2, top2, taken2 = _ranked_top(s2, index, exact_ties, want_rank=True)
        cnt, z, taken = _pair_counts(top1, top2, exact_ties)
        c1 = jnp.zeros(s1.shape, F32)
        for a in range(PEER_TOPK):
            is_a = (rank1 == float(a)) if exact_ties else (s1 == top1[a])
            c1 = jnp.where(is_a, cnt[a:a + 1], c1)
        c1_ref[:, cols] = c1
        p1_ref[:, cols] = jnp.exp(s1 - top1[0]) * (1.0 / z)
        r2_ref[:, cols] = rank2.astype(r2_ref.dtype)
        p2_ref[:, cols] = jnp.exp(s2 - top2[0]).astype(p2_ref.dtype)
        for n_taken in (taken1, taken2, taken):
            tied = jnp.maximum(tied, jnp.abs(n_taken - float(PEER_TOPK)))
        tied_per_strip.append(tied)
    return tied_per_strip


def _topk_kernel(s_ref, c1_ref, p1_ref, r2_ref, p2_ref):
    strips = range(s_ref.shape[1] // LANE)
    tied = _topk_strips(s_ref, c1_ref, p1_ref, r2_ref, p2_ref, exact_ties=False, strips=strips)

    for strip in strips:
        @pl.when(jnp.max(tied[strip]) > 0.0)
        def _(strip=strip):
            _topk_strips(s_ref, c1_ref, p1_ref, r2_ref, p2_ref, exact_ties=True, strips=(strip,))


def _peer_topk(s_t, ph, nk, tk):
    t = s_t.shape[1]
    out_spec = pl.BlockSpec((nk, tk), lambda h, i: (h, i))
    return pl.pallas_call(
        _topk_kernel,
        grid=(ph, t // tk),
        in_specs=[pl.BlockSpec((2 * nk, tk), lambda h, i: (h, i))],
        out_specs=[out_spec] * 4,
        out_shape=[jax.ShapeDtypeStruct((ph * nk, t), dt) for dt in (F32, F32, BF16, BF16)],
        compiler_params=_params("parallel", "parallel"),
        name="peer_topk",
    )(s_t)


def _gelu_tanh(x):
    c = math.sqrt(2.0 / math.pi)
    z2 = x * (2.0 * c + (2.0 * c * 0.044715) * (x * x))
    return x / (1.0 + jnp.exp(-z2))


def _peer_ffn_kernel(h_ref, u_ref, vt_ref, c1_ref, p1_ref, r2_ref, p2_ref, o_ref, act_scr, bc_scr, *, nk):
    j = pl.program_id(1)
    te = u_ref.shape[0]
    tt = h_ref.shape[0]
    heads = c1_ref.shape[0]
    keys_per_step = te // nk
    steps_per_tile = SUBLANE // keys_per_step

    @pl.when(j == 0)
    def _():
        o_ref[...] = jnp.zeros_like(o_ref)

    rows = BF16_SUBLANES
    zero = jnp.zeros((rows, tt), BF16)
    keys_per_dot = max(keys_per_step // PEER_SCORE_DOTS, 1)
    for e in range(keys_per_step):
        if e % keys_per_dot == 0:
            a = lax.dot_general(u_ref[e * nk:(e + keys_per_dot) * nk, :], h_ref[...], NT_DIMS,
                                preferred_element_type=F32)
        row = (j % steps_per_tile) * keys_per_step + e
        for h in range(heads):
            bc_scr[e, h, 0] = jnp.broadcast_to(c1_ref[h, pl.ds(row, 1), :].astype(BF16), (rows, tt))
            bc_scr[e, h, 1] = jnp.broadcast_to(p1_ref[h, pl.ds(row, 1), :].astype(BF16), (rows, tt))
        for rc in range(nk // rows):
            gate = zero
            for h in range(heads):
                r2 = r2_ref[h * nk + rc * rows:h * nk + (rc + 1) * rows, :]
                p2 = p2_ref[h * nk + rc * rows:h * nk + (rc + 1) * rows, :]
                gate = gate + jnp.where(r2 < bc_scr[e, h, 0], bc_scr[e, h, 1] * p2, zero)
            dst = slice(e * nk + rc * rows, e * nk + (rc + 1) * rows)
            src = slice((e % keys_per_dot) * nk + rc * rows, (e % keys_per_dot) * nk + (rc + 1) * rows)
            act_scr[dst, :] = _gelu_tanh(a[src, :]).astype(BF16) * gate
    o_ref[...] += jnp.dot(vt_ref[...], act_scr[...], preferred_element_type=F32)


def _peer_ffn(h2, u_bf16, vt_blocks, tables, heads, nk, tt):
    t, d = h2.shape
    n_chunks, _, te = vt_blocks.shape
    keys_per_step = te // nk
    assert SUBLANE % keys_per_step == 0
    steps_per_tile = SUBLANE // keys_per_step
    c1, p1, r2, p2 = tables
    row_spec = pl.BlockSpec((heads, SUBLANE, tt), lambda i, j: (0, j // steps_per_tile, i))
    once = pl.Buffered(1)
    tab_spec = pl.BlockSpec((heads * nk, tt), lambda i, j: (0, i), pipeline_mode=once)
    return pl.pallas_call(
        functools.partial(_peer_ffn_kernel, nk=nk),
        grid=(t // tt, n_chunks),
        in_specs=[pl.BlockSpec((tt, d), lambda i, j: (i, 0), pipeline_mode=once),
                  pl.BlockSpec((te, d), lambda i, j: (j, 0)),
                  pl.BlockSpec((None, d, te), lambda i, j: (j, 0, 0)),
                  row_spec, row_spec, tab_spec, tab_spec],
        out_specs=pl.BlockSpec((d, tt), lambda i, j: (0, i), pipeline_mode=once),
        out_shape=jax.ShapeDtypeStruct((d, t), F32),
        scratch_shapes=[pltpu.VMEM((te, tt), BF16),
                        pltpu.VMEM((keys_per_step, heads, 2, BF16_SUBLANES, tt), BF16)],
        compiler_params=_params("parallel", "arbitrary"),
        name="peer_ffn",
    )(h2, u_bf16, vt_blocks, c1.reshape(heads, nk, t), p1.reshape(heads, nk, t), r2, p2)


def _residual_kernel(x_ref, pt_ref, gt_ref, g_ref, o_ref, *, final_norm):
    x = x_ref[...] + gt_ref[...] * jnp.transpose(pt_ref[...])
    if final_norm:
        x = (x * lax.rsqrt(jnp.mean(x * x, axis=-1, keepdims=True) + RMS_EPS)) * g_ref[...]
    o_ref[...] = x


def _residual(xt, peer_t, gate, g_final, seq, tm, final_norm):
    t, d = xt.shape
    per_batch = seq // tm
    return pl.pallas_call(
        functools.partial(_residual_kernel, final_norm=final_norm),
        grid=(t // tm,),
        in_specs=[pl.BlockSpec((tm, d), lambda i: (i, 0)),
                  pl.BlockSpec((d, tm), lambda i: (0, i)),
                  pl.BlockSpec((None, 1, d), lambda i: (i // per_batch, 0, 0)),
                  pl.BlockSpec((1, d), lambda i: (0, 0))],
        out_specs=pl.BlockSpec((tm, d), lambda i: (i, 0)),
        out_shape=jax.ShapeDtypeStruct((t, d), F32),
        compiler_params=_params("parallel"),
        name="residual",
    )(xt, peer_t, gate, g_final.reshape(1, d))


def kernel(x, c, w_ada, b_ada, g_mix, w_in, rel_bias, pool_w, pool_scale, w_out, g_ffn, peer_wq, peer_subkeys,
           peer_u, peer_v, g_final):
    batch, seq, d = x.shape
    t = batch * seq
    depth = w_ada.shape[0]
    pool_total = pool_scale.shape[-1]
    att_w = (w_in.shape[-1] - pool_total) // 3
    heads = att_w // HEAD_DIM
    ph, _, nk, _ = peer_subkeys.shape[1:]
    n_exp = peer_u.shape[1]
    tm = min(512, seq)
    tn = min(1024, d)
    te = 8 * nk

    xt = x.reshape(t, d)
    bias_tables = _attention_bias_tables(rel_bias)

    for layer in range(depth):
        mod = _ada(c, w_ada, b_ada, layer)
        sh1, sc1, gt1, sh2, sc2, gt2 = [m.reshape(batch, 1, d) for m in jnp.split(mod, 6, axis=-1)]

        proj_cm = _inproj(xt, g_mix, sc1, sh1, w_in[layer].astype(BF16), layer, seq, tm, tn)
        attn_hm = _attention(proj_cm, bias_tables, batch, seq, heads)
        xt = _outproj(attn_hm, proj_cm, pool_w[layer].astype(BF16), pool_scale, w_out[layer].astype(BF16),
                      xt, gt1, layer, seq, heads, tm, tn)

        h2, s_t = _peer_q(xt, g_ffn, sc2, sh2, peer_wq[layer].astype(BF16),
                          peer_subkeys[layer].astype(BF16), layer, seq, tm)
        tables = _peer_topk(s_t, ph, nk, tk=min(512, t))
        vt_blocks = jnp.transpose(peer_v[layer].reshape(n_exp // te, te, d), (0, 2, 1)).astype(BF16)
        peer_t = _peer_ffn(h2, peer_u[layer].astype(BF16), vt_blocks, tables, ph, nk, tt=min(512, t))
        xt = _residual(xt, peer_t, gt2, g_final, seq, min(256, seq), final_norm=(layer == depth - 1))

    return xt.reshape(batch, seq, d)
```
